```python
import jax
import jax.numpy as jnp
from jax import lax
import numpy as np

D_MODEL = 4096
BATCH = 4
SEQ = 4096
DEPTH = 1
DEC_BATCH = 32
DEC_SEQ = 32
PAST_LEN = 1024

CHUNK = 64
D_MIX = D_MODEL
GLA_HEADS = 8
GLA_VAL_W = D_MIX // 2
GLA_DV = GLA_VAL_W // GLA_HEADS
GLA_DK = GLA_DV // 2
GLA_KEY_W = GLA_HEADS * GLA_DK
GATE_RANK = 16
GATE_NORMALIZER = 16.0
CONV_W = D_MIX - GLA_VAL_W
CONV_K = 3
N_EXPERTS = 256
TOP_K = 8
N_GROUPS = 8
TOPK_GROUPS = 4
D_EXPERT = D_MODEL // 4
ROUTED_SCALE = 2.5
MOE_BLOCK = 256
EPS = 1e-6

PROJ_SPLITS = (
    GLA_KEY_W,
    2 * GLA_KEY_W,
    2 * GLA_KEY_W + GLA_VAL_W,
    2 * GLA_KEY_W + 2 * GLA_VAL_W,
    2 * GLA_KEY_W + 2 * GLA_VAL_W + GATE_RANK,
    2 * GLA_KEY_W + 2 * GLA_VAL_W + GATE_RANK + CONV_W,
    2 * GLA_KEY_W + 2 * GLA_VAL_W + GATE_RANK + 2 * CONV_W,
)
PROJ_W = 2 * GLA_KEY_W + 2 * GLA_VAL_W + GATE_RANK + 3 * CONV_W

kernel_name = 'hymba_gla_shortconv_moe_stream_step'


def rms_norm(x, gain):
    xf = x.astype(jnp.float32)
    y = xf * lax.rsqrt(jnp.mean(xf * xf, axis=-1, keepdims=True) + EPS)
    return (y * gain.astype(jnp.float32)).astype(x.dtype)


def ada_modulation(c, w_ada, b_ada):
    mod = jax.nn.silu(c) @ w_ada + b_ada
    return jnp.split(mod[:, None, :], 6, axis=-1)


def gla_scan(q, k, v, log_a, s0):
    B, H, T, _ = q.shape
    L = min(CHUNK, T)
    n = T // L

    def blocks(a):
        return jnp.moveaxis(a.astype(jnp.float32).reshape(B, H, n, L, a.shape[-1]), 2, 0)

    qb = blocks(q) * (GLA_DK ** -0.5)
    kb, vb, gb = blocks(k), blocks(v), blocks(log_a)
    causal = jnp.tril(jnp.ones((L, L), dtype=bool))[:, :, None]

    def step(S, blk):
        qc, kc, vc, gc = blk
        bc = jnp.cumsum(gc, axis=-2)
        diff = bc[..., :, None, :] - bc[..., None, :, :]
        decay = jnp.exp(jnp.where(causal, diff, -jnp.inf))
        att = jnp.einsum('bhtd,bhsd,bhtsd->bhts', qc, kc, decay)
        o = att @ vc + jnp.einsum('bhtd,bhdv->bhtv', qc * jnp.exp(bc), S)
        b_last = bc[..., -1:, :]
        S = (jnp.exp(b_last[..., 0, :])[..., None] * S
             + jnp.einsum('bhsd,bhsv->bhdv', kc * jnp.exp(b_last - bc), vc))
        return S, o

    S, o = lax.scan(step, s0.astype(jnp.float32), (qb, kb, vb, gb))
    o = jnp.moveaxis(o, 0, 2).reshape(B, H, T, -1)
    return o, S


def token_mixers(h, s_gla, s_conv, w_in, w_decay_up, b_decay_up, gla_norm, conv_w, w_out):
    B, T, _ = h.shape
    proj = h @ w_in
    q, k, v, r, a_lr, hc, cg, bg = jnp.split(proj, PROJ_SPLITS, axis=-1)

    def heads(a):
        return a.reshape(B, T, GLA_HEADS, -1).transpose(0, 2, 1, 3)

    log_a = jax.nn.log_sigmoid((a_lr @ w_decay_up + b_decay_up).astype(jnp.float32)) / GATE_NORMALIZER
    o, s_gla_new = gla_scan(heads(q), heads(k), heads(v), heads(log_a), s_gla)
    o = rms_norm(o.transpose(0, 2, 1, 3), gla_norm)
    o = (o * jax.nn.silu(r).reshape(B, T, GLA_HEADS, GLA_DV).astype(jnp.float32))
    o = o.reshape(B, T, GLA_VAL_W).astype(h.dtype)

    u = cg * hc
    up = jnp.concatenate([s_conv.astype(u.dtype), u], axis=1)
    y = conv_w[0] * up[:, 0:T]
    for j in range(1, CONV_K):
        y = y + conv_w[j] * up[:, j:j + T]
    conv_out = bg * y

    out = jnp.concatenate([o, conv_out], axis=-1) @ w_out
    return out, s_gla_new, up[:, -(CONV_K - 1):]


def mixing_sublayer(x, shift, scale, gate, s_gla, s_conv, pre, post,
                    w_in, w_decay_up, b_decay_up, gla_norm, conv_w, w_out):
    h = rms_norm(x, pre) * (1 + scale) + shift
    out, s_gla_new, s_conv_new = token_mixers(h, s_gla, s_conv, w_in, w_decay_up, b_decay_up,
                                              gla_norm, conv_w, w_out)
    return x + gate * rms_norm(out, post), s_gla_new, s_conv_new


def swiglu(x, w_gate, w_up, w_down):
    return (jax.nn.silu(x @ w_gate) * (x @ w_up)) @ w_down


def route(t, w_router, b_router):
    n = t.shape[0]
    scores = jax.nn.sigmoid((t @ w_router).astype(jnp.float32))
    choice = scores + b_router.astype(jnp.float32)
    grp = choice.reshape(n, N_GROUPS, N_EXPERTS // N_GROUPS)
    grp_score = lax.top_k(grp, 2)[0].sum(-1)
    top_g = lax.top_k(grp_score, TOPK_GROUPS)[1]
    gmask = jnp.any(top_g[:, :, None] == jnp.arange(N_GROUPS)[None, None, :], axis=1)
    emask = jnp.repeat(gmask, N_EXPERTS // N_GROUPS, axis=1)
    top_e = lax.top_k(jnp.where(emask, choice, -jnp.inf), TOP_K)[1]
    w = jnp.take_along_axis(scores, top_e, axis=1)
    w = w / (jnp.sum(w, axis=-1, keepdims=True) + 1e-20) * ROUTED_SCALE
    return top_e, w


def routed_experts(t, top_e, gate_w, w_eg, w_eu, w_ed, layer):
    n, d = t.shape
    nk = n * TOP_K
    flat_e = top_e.reshape(nk).astype(jnp.int32)
    flat_tok = jnp.repeat(jnp.arange(n, dtype=jnp.int32), TOP_K)
    flat_w = gate_w.reshape(nk)
    order = jnp.argsort(flat_e)
    se, stok, sw = flat_e[order], flat_tok[order], flat_w[order]
    counts = jnp.bincount(flat_e, length=N_EXPERTS)
    starts = jnp.cumsum(counts) - counts
    padded = (counts + MOE_BLOCK - 1) // MOE_BLOCK * MOE_BLOCK
    pends = jnp.cumsum(padded)
    dest = (pends - padded)[se] + jnp.arange(nk, dtype=jnp.int32) - starts[se]
    n_blocks = -(-(nk + N_EXPERTS * (MOE_BLOCK - 1)) // MOE_BLOCK)
    rows = n_blocks * MOE_BLOCK
    row_tok = jnp.full((rows,), n, jnp.int32).at[dest].set(stok)
    row_w = jnp.zeros((rows,), jnp.float32).at[dest].set(sw)
    block_e = jnp.minimum(
        jnp.searchsorted(pends, jnp.arange(n_blocks, dtype=jnp.int32) * MOE_BLOCK, side='right'),
        N_EXPERTS - 1)
    t_ext = jnp.concatenate([t, jnp.zeros((1, d), t.dtype)], axis=0)

    def block(acc, blk):
        e, toks, ws = blk
        y = swiglu(t_ext[toks], w_eg[layer, e], w_eu[layer, e], w_ed[layer, e])
        return acc.at[toks].add(y.astype(jnp.float32) * ws[:, None]), None

    acc, _ = lax.scan(block, jnp.zeros((n + 1, d), jnp.float32),
                      (block_e, row_tok.reshape(n_blocks, MOE_BLOCK), row_w.reshape(n_blocks, MOE_BLOCK)))
    return acc[:n].astype(t.dtype)


def setup_inputs(seed: int = 0) -> dict:
    key = jax.random.key(seed)
    ks = jax.random.split(key, 28)

    def nrm(k, shape, scale=1.0):
        return jax.random.normal(k, shape, jnp.float32) * scale

    def gain(k, shape):
        return 1.0 + 0.05 * jax.random.normal(k, shape, jnp.float32)

    return {
        'x_prompt': nrm(ks[0], (BATCH, SEQ, D_MODEL)),
        'x_sample': nrm(ks[1], (DEC_BATCH, DEC_SEQ, D_MODEL)),
        'c_prompt': nrm(ks[2], (BATCH, D_MODEL)),
        'c_sample': nrm(ks[3], (DEC_BATCH, D_MODEL)),
        'state_gla': nrm(ks[4], (DEPTH, DEC_BATCH, GLA_HEADS, GLA_DK, GLA_DV)),
        'state_conv': nrm(ks[5], (DEPTH, DEC_BATCH, CONV_K - 1, CONV_W)),
        'w_ada': nrm(ks[6], (DEPTH, D_MODEL, 6 * D_MODEL), 0.5 * D_MODEL ** -0.5),
        'b_ada': nrm(ks[7], (DEPTH, 6 * D_MODEL), 0.02),
        'norm_mix_pre': gain(ks[8], (DEPTH, D_MODEL)),
        'norm_mix_post': gain(ks[9], (DEPTH, D_MODEL)),
        'norm_ffn_pre': gain(ks[10], (DEPTH, D_MODEL)),
        'norm_ffn_post': gain(ks[11], (DEPTH, D_MODEL)),
        'w_in': nrm(ks[12], (DEPTH, D_MODEL, PROJ_W), D_MODEL ** -0.5),
        'w_decay_up': nrm(ks[13], (DEPTH, GATE_RANK, GLA_KEY_W), GATE_RANK ** -0.5),
        'b_decay_up': nrm(ks[14], (DEPTH, GLA_KEY_W), 0.1),
        'gla_norm': gain(ks[15], (DEPTH, GLA_DV)),
        'conv_w': nrm(ks[16], (DEPTH, CONV_K, CONV_W), CONV_K ** -0.5),
        'w_out': nrm(ks[17], (DEPTH, D_MIX, D_MODEL), D_MIX ** -0.5),
        'w_router': nrm(ks[18], (DEPTH, D_MODEL, N_EXPERTS), D_MODEL ** -0.5),
        'b_router': nrm(ks[19], (DEPTH, N_EXPERTS), 0.01),
        'w_shared_gate': nrm(ks[20], (DEPTH, D_MODEL, D_EXPERT), D_MODEL ** -0.5),
        'w_shared_up': nrm(ks[21], (DEPTH, D_MODEL, D_EXPERT), D_MODEL ** -0.5),
        'w_shared_down': nrm(ks[22], (DEPTH, D_EXPERT, D_MODEL), D_EXPERT ** -0.5),
        'w_expert_gate': nrm(ks[23], (DEPTH, N_EXPERTS, D_MODEL, D_EXPERT), D_MODEL ** -0.5),
        'w_expert_up': nrm(ks[24], (DEPTH, N_EXPERTS, D_MODEL, D_EXPERT), D_MODEL ** -0.5),
        'w_expert_down': nrm(ks[25], (DEPTH, N_EXPERTS, D_EXPERT, D_MODEL), D_EXPERT ** -0.5),
    }


def reference(x_prompt, x_sample, c_prompt, c_sample, state_gla, state_conv,
              w_ada, b_ada, norm_mix_pre, norm_mix_post, norm_ffn_pre, norm_ffn_post,
              w_in, w_decay_up, b_decay_up, gla_norm, conv_w, w_out,
              w_router, b_router, w_shared_gate, w_shared_up, w_shared_down,
              w_expert_gate, w_expert_up, w_expert_down):
    xp, xs = x_prompt, x_sample
    bp, tp, _ = xp.shape
    bs, ts, _ = xs.shape
    gla_p_list, conv_p_list, gla_s_list, conv_s_list = [], [], [], []
    for layer in range(DEPTH):
        mod_p = ada_modulation(c_prompt, w_ada[layer], b_ada[layer])
        mod_s = ada_modulation(c_sample, w_ada[layer], b_ada[layer])
        mix_w = (norm_mix_pre[layer], norm_mix_post[layer], w_in[layer], w_decay_up[layer],
                 b_decay_up[layer], gla_norm[layer], conv_w[layer], w_out[layer])

        xp, gla_p, conv_p = mixing_sublayer(
            xp, mod_p[0], mod_p[1], mod_p[2],
            jnp.zeros((bp, GLA_HEADS, GLA_DK, GLA_DV), jnp.float32),
            jnp.zeros((bp, CONV_K - 1, CONV_W), xp.dtype), *mix_w)
        xs, gla_s, conv_s = mixing_sublayer(
            xs, mod_s[0], mod_s[1], mod_s[2], state_gla[layer], state_conv[layer], *mix_w)

        hp = rms_norm(xp, norm_ffn_pre[layer]) * (1 + mod_p[4]) + mod_p[3]
        hs = rms_norm(xs, norm_ffn_pre[layer]) * (1 + mod_s[4]) + mod_s[3]
        tokens = jnp.concatenate([hp.reshape(-1, D_MODEL), hs.reshape(-1, D_MODEL)], axis=0)
        top_e, gate_w = route(tokens, w_router[layer], b_router[layer])
        f = (swiglu(tokens, w_shared_gate[layer], w_shared_up[layer], w_shared_down[layer])
             + routed_experts(tokens, top_e, gate_w, w_expert_gate, w_expert_up, w_expert_down, layer))
        fp = f[:bp * tp].reshape(bp, tp, D_MODEL)
        fs = f[bp * tp:].reshape(bs, ts, D_MODEL)
        xp = xp + mod_p[5] * rms_norm(fp, norm_ffn_post[layer])
        xs = xs + mod_s[5] * rms_norm(fs, norm_ffn_post[layer])

        gla_p_list.append(gla_p.astype(x_prompt.dtype))
        conv_p_list.append(conv_p)
        gla_s_list.append(gla_s.astype(state_gla.dtype))
        conv_s_list.append(conv_s.astype(state_conv.dtype))

    return (xp, xs, jnp.stack(gla_p_list), jnp.stack(conv_p_list),
            jnp.stack(gla_s_list), jnp.stack(conv_s_list))
```

```python
import functools

import jax
import jax.numpy as jnp
from jax import lax
from jax.experimental import pallas as pl
from jax.experimental.pallas import tpu as pltpu

CHUNK = 64
TOP_K = 8
N_GROUPS = 8
TOPK_GROUPS = 4
ROUTED_SCALE = 2.5
EPS = 1e-6
GATE_NORMALIZER = 16.0

LANES = 128
SUBLANES = 8
VMEM_LIMIT_BYTES = 56 * 1024 * 1024

MOE_SUB = 256
MOE_SUBS_PER_ITEM = 3

F32 = jnp.float32
BF16 = jnp.bfloat16
HIGHEST = lax.Precision.HIGHEST
NT_DIMS = (((1,), (1,)), ((), ()))
TN_DIMS = (((0,), (0,)), ((), ()))


def _tile(n, pref, mult=SUBLANES):
    t = min(pref, n)
    t -= t % mult
    while t >= mult:
        if n % t == 0:
            return t
        t -= mult
    return n


def _aligned(x, m):
    return x if isinstance(x, int) else pl.multiple_of(x, m)


def _params(*sem):
    return pltpu.CompilerParams(dimension_semantics=sem, vmem_limit_bytes=VMEM_LIMIT_BYTES)


def _silu(x):
    return x * jax.nn.sigmoid(x)


def _rms(x):
    return x * lax.rsqrt(jnp.mean(x * x, axis=-1, keepdims=True) + EPS)


def _ada_kernel(c_ref, w_ref, b_ref, o_ref):
    s = _silu(c_ref[...]).astype(BF16)
    o_ref[...] = jnp.dot(s, w_ref[...].astype(BF16), preferred_element_type=F32) + b_ref[...]


def _ada(c_all, w_ada, b_ada):
    rows, d = c_all.shape
    n = w_ada.shape[1]
    tn = _tile(n, 512, LANES)
    return pl.pallas_call(
        _ada_kernel,
        grid=(n // tn,),
        in_specs=[pl.BlockSpec((rows, d), lambda j: (0, 0)),
                  pl.BlockSpec((d, tn), lambda j: (0, j)),
                  pl.BlockSpec((1, tn), lambda j: (0, j))],
        out_specs=pl.BlockSpec((rows, tn), lambda j: (0, j)),
        out_shape=jax.ShapeDtypeStruct((rows, n), F32),
        compiler_params=_params("arbitrary"),
        name="ada",
    )(c_all, w_ada, b_ada)


def _inproj_kernel(x_ref, sc_ref, sh_ref, g_ref, w_ref, wa_ref, p_ref, a_ref, h_scr):
    @pl.when(pl.program_id(1) == 0)
    def _():
        h = _rms(x_ref[...]) * g_ref[...] * (1.0 + sc_ref[...]) + sh_ref[...]
        hb = h.reshape(h_scr.shape).astype(BF16)
        h_scr[...] = hb
        a_ref[...] = jnp.dot(hb, wa_ref[...], preferred_element_type=F32)

    p_ref[...] = jnp.dot(h_scr[...], w_ref[...], preferred_element_type=F32).astype(p_ref.dtype)


def _inproj(x_u, mod_u, gain, w_main, w_a):
    nu, unit, d = x_u.shape
    npj = w_main.shape[1]
    tu = _tile(nu, 16, 1)
    tm = tu * unit
    tn = _tile(npj, 1024, LANES)
    return pl.pallas_call(
        _inproj_kernel,
        grid=(nu // tu, npj // tn),
        in_specs=[pl.BlockSpec((tu, unit, d), lambda i, j: (i, 0, 0)),
                  pl.BlockSpec((None, tu, 1, d), lambda i, j: (1, i, 0, 0)),
                  pl.BlockSpec((None, tu, 1, d), lambda i, j: (0, i, 0, 0)),
                  pl.BlockSpec((1, d), lambda i, j: (0, 0)),
                  pl.BlockSpec((d, tn), lambda i, j: (0, j)),
                  pl.BlockSpec((d, LANES), lambda i, j: (0, 0))],
        out_specs=[pl.BlockSpec((tm, tn), lambda i, j: (i, j)),
                   pl.BlockSpec((tm, LANES), lambda i, j: (i, 0))],
        out_shape=[jax.ShapeDtypeStruct((nu * unit, npj), BF16),
                   jax.ShapeDtypeStruct((nu * unit, LANES), F32)],
        scratch_shapes=[pltpu.VMEM((tm, d), BF16)],
        compiler_params=_params("parallel", "arbitrary"),
        name="inproj",
    )(x_u, mod_u, mod_u, gain, w_main, w_a)


def _gla_kernel(q_ref, k_ref, v_ref, r_ref, a_ref, wup_ref, bup_ref, gn_ref, s0_ref, mix_ref,
                o_ref, sout_ref, st_scr, *, chunk, nchunk, dk):
    del mix_ref
    t = pl.program_id(2)

    @pl.when(t == 0)
    def _():
        st_scr[...] = s0_ref[0, 0].T

    row = lax.broadcasted_iota(jnp.int32, (chunk, chunk), 0)
    col = lax.broadcasted_iota(jnp.int32, (chunk, chunk), 1)
    causal = row >= col
    tri = causal.astype(F32)
    mid = chunk // 2
    qscale = dk ** -0.5

    def body(c, carry):
        sl = pl.ds(pl.multiple_of(c * chunk, chunk), chunk)
        gp = jnp.dot(a_ref[sl, :], wup_ref[...], precision=HIGHEST,
                     preferred_element_type=F32) + bup_ref[...]
        g = (jnp.minimum(gp, 0.0) - jnp.log(1.0 + jnp.exp(-jnp.abs(gp)))) / GATE_NORMALIZER
        bc = jnp.dot(tri, g, precision=HIGHEST, preferred_element_type=F32)
        m = bc[mid:mid + 1, :]
        bl = bc[chunk - 1:chunk, :]
        e_q = jnp.exp(bc - m)
        e_k = jnp.exp(m - bc)
        e_m = jnp.exp(m)
        e_l = jnp.exp(bl - m)
        qe = q_ref[sl, :].astype(F32) * qscale * e_q
        ke = k_ref[sl, :].astype(F32) * e_k
        v = v_ref[sl, :]
        att = lax.dot_general(qe.astype(BF16), ke.astype(BF16), NT_DIMS, preferred_element_type=F32)
        att = jnp.where(causal, att, 0.0)
        st = st_scr[...]
        o = jnp.dot(att.astype(BF16), v, preferred_element_type=F32)
        o = o + lax.dot_general((qe * e_m).astype(BF16), st.astype(BF16), NT_DIMS,
                                preferred_element_type=F32)
        upd = lax.dot_general(v, (ke * e_l).astype(BF16), TN_DIMS, preferred_element_type=F32)
        st_scr[...] = st * (e_m * e_l) + upd
        r = r_ref[sl, :].astype(F32)
        o_ref[sl, :] = (_rms(o) * gn_ref[...] * _silu(r)).astype(o_ref.dtype)
        return carry

    lax.fori_loop(0, nchunk, body, 0)

    @pl.when(t == pl.num_programs(2) - 1)
    def _():
        sout_ref[0, 0] = st_scr[...].T


def _gla(proj, a_lr, w_up, b_up, gn, s0, mix, *, row0, nb, t_len, heads, dk, dv):
    chunk = min(CHUNK, t_len)
    tt = _tile(t_len, 8 * chunk, chunk)
    nt = t_len // tt
    rb0 = row0 // tt
    kq, kk, kv, kr = 0, heads, (2 * heads * dk) // dv, (2 * heads * dk) // dv + heads

    def rows(b, h, t):
        return rb0 + b * nt + t

    in_specs = [pl.BlockSpec((tt, dk), lambda b, h, t: (rows(b, h, t), kq + h)),
                pl.BlockSpec((tt, dk), lambda b, h, t: (rows(b, h, t), kk + h)),
                pl.BlockSpec((tt, dv), lambda b, h, t: (rows(b, h, t), kv + h)),
                pl.BlockSpec((tt, dv), lambda b, h, t: (rows(b, h, t), kr + h)),
                pl.BlockSpec((tt, LANES), lambda b, h, t: (rows(b, h, t), 0)),
                pl.BlockSpec((LANES, dk), lambda b, h, t: (0, h)),
                pl.BlockSpec((1, dk), lambda b, h, t: (0, h)),
                pl.BlockSpec((1, dv), lambda b, h, t: (0, 0)),
                pl.BlockSpec((1, 1, dk, dv), lambda b, h, t: (b, h, 0, 0)),
                pl.BlockSpec(memory_space=pl.ANY)]
    return pl.pallas_call(
        functools.partial(_gla_kernel, chunk=chunk, nchunk=tt // chunk, dk=dk),
        grid=(nb, heads, nt),
        in_specs=in_specs,
        out_specs=[pl.BlockSpec((tt, dv), lambda b, h, t: (rows(b, h, t), h)),
                   pl.BlockSpec((1, 1, dk, dv), lambda b, h, t: (b, h, 0, 0))],
        out_shape=[jax.ShapeDtypeStruct(mix.shape, mix.dtype),
                   jax.ShapeDtypeStruct((nb, heads, dk, dv), F32)],
        scratch_shapes=[pltpu.VMEM((dv, dk), F32)],
        input_output_aliases={9: 0},
        compiler_params=_params("parallel", "parallel", "arbitrary"),
        name="gla",
    )(proj, proj, proj, proj, a_lr, w_up, b_up, gn, s0, mix)


def _conv_kernel(hc_ref, cg_ref, bg_ref, w_ref, c0_ref, mix_ref, y_ref, st_ref, carry_scr):
    del mix_ref
    t = pl.program_id(2)

    @pl.when(t == 0)
    def _():
        carry_scr[...] = c0_ref[0]

    u = cg_ref[...].astype(F32) * hc_ref[...].astype(F32)
    tt = u.shape[0]
    row = lax.broadcasted_iota(jnp.int32, u.shape, 0)
    prev = carry_scr[...]
    p2 = prev[SUBLANES - 2:SUBLANES - 1, :]
    p1 = prev[SUBLANES - 1:SUBLANES, :]
    u1 = jnp.where(row == 0, p1, pltpu.roll(u, 1, 0))
    u2 = jnp.where(row == 0, p2, jnp.where(row == 1, p1, pltpu.roll(u, 2, 0)))
    w = w_ref[...]
    y = w[0:1, :] * u2 + w[1:2, :] * u1 + w[2:3, :] * u
    y_ref[...] = (bg_ref[...].astype(F32) * y).astype(y_ref.dtype)
    tail = u[tt - SUBLANES:tt, :]
    carry_scr[...] = tail

    @pl.when(t == pl.num_programs(2) - 1)
    def _():
        st_ref[0] = tail


def _conv(proj, conv_w8, c0, mix, *, row0, nb, t_len, col0, width, val_w):
    tt = _tile(t_len, 512)
    tc = _tile(width, 1024, LANES)
    nt = t_len // tt
    rb0 = row0 // tt
    cb0 = col0 // tc
    ncb = width // tc

    def rows(b, t):
        return rb0 + b * nt + t

    return pl.pallas_call(
        _conv_kernel,
        grid=(nb, ncb, nt),
        in_specs=[pl.BlockSpec((tt, tc), lambda b, j, t: (rows(b, t), cb0 + j)),
                  pl.BlockSpec((tt, tc), lambda b, j, t: (rows(b, t), cb0 + ncb + j)),
                  pl.BlockSpec((tt, tc), lambda b, j, t: (rows(b, t), cb0 + 2 * ncb + j)),
                  pl.BlockSpec((SUBLANES, tc), lambda b, j, t: (0, j)),
                  pl.BlockSpec((1, SUBLANES, tc), lambda b, j, t: (b, 0, j)),
                  pl.BlockSpec(memory_space=pl.ANY)],
        out_specs=[pl.BlockSpec((tt, tc), lambda b, j, t: (rows(b, t), val_w // tc + j)),
                   pl.BlockSpec((1, SUBLANES, tc), lambda b, j, t: (b, 0, j))],
        out_shape=[jax.ShapeDtypeStruct(mix.shape, mix.dtype),
                   jax.ShapeDtypeStruct((nb, SUBLANES, width), F32)],
        scratch_shapes=[pltpu.VMEM((SUBLANES, tc), F32)],
        input_output_aliases={5: 0},
        compiler_params=_params("parallel", "parallel", "arbitrary"),
        name="conv",
    )(proj, proj, proj, conv_w8, c0, mix)


def _outproj_kernel(mix_ref, w_ref, x_ref, gate_ref, post_ref, pre2_ref, sc2_ref, sh2_ref,
                    x1_ref, h2_ref, acc):
    kk = pl.program_id(1)

    @pl.when(kk == 0)
    def _():
        acc[...] = jnp.zeros_like(acc)

    acc[...] += jnp.dot(mix_ref[...], w_ref[...], preferred_element_type=F32)

    @pl.when(kk == pl.num_programs(1) - 1)
    def _():
        out = acc[...].reshape(x_ref.shape)
        x1 = x_ref[...] + gate_ref[...] * (_rms(out) * post_ref[...])
        x1_ref[...] = x1
        h2 = _rms(x1) * pre2_ref[...] * (1.0 + sc2_ref[...]) + sh2_ref[...]
        h2 = h2.reshape(acc.shape)
        for c in range(h2_ref.shape[1]):
            h2_ref[:, c, :] = h2[:, c * LANES:(c + 1) * LANES]


def _outproj(mix, w_out, x_u, mod_u, post, pre2):
    nu, unit, d = x_u.shape
    kdim = mix.shape[1]
    tu = _tile(nu, 8, 1)
    tm = tu * unit
    tk = _tile(kdim, 1024, LANES)

    def mod(comp):
        return pl.BlockSpec((None, tu, 1, d), lambda i, k: (comp, i, 0, 0))

    vec = pl.BlockSpec((1, d), lambda i, k: (0, 0))
    return pl.pallas_call(
        _outproj_kernel,
        grid=(nu // tu, kdim // tk),
        in_specs=[pl.BlockSpec((tm, tk), lambda i, k: (i, k)),
                  pl.BlockSpec((tk, d), lambda i, k: (k, 0)),
                  pl.BlockSpec((tu, unit, d), lambda i, k: (i, 0, 0)),
                  mod(2), vec, vec, mod(4), mod(3)],
        out_specs=[pl.BlockSpec((tu, unit, d), lambda i, k: (i, 0, 0)),
                   pl.BlockSpec((tm, d // LANES, LANES), lambda i, k: (i, 0, 0))],
        out_shape=[jax.ShapeDtypeStruct((nu, unit, d), F32),
                   jax.ShapeDtypeStruct((nu * unit, d // LANES, LANES), F32)],
        scratch_shapes=[pltpu.VMEM((tm, d), F32)],
        compiler_params=_params("parallel", "arbitrary"),
        name="outproj",
    )(mix, w_out, x_u, mod_u, post, pre2, mod_u, mod_u)


def _rows_to_matrix(ref, rows=None):
    sl = slice(None) if rows is None else rows
    return jnp.concatenate([ref[sl, c, :] for c in range(ref.shape[1])], axis=-1)


def _route_kernel(h_ref, wr_ref, br_ref, e_ref, p_ref, w_ref, cnt_ref, cnt_scr, *, n_exp):
    i = pl.program_id(0)

    @pl.when(i == 0)
    def _():
        cnt_scr[...] = jnp.zeros_like(cnt_scr)

    x = _rows_to_matrix(h_ref)
    tm = x.shape[0]
    logits = jnp.dot(x, wr_ref[...], precision=HIGHEST, preferred_element_type=F32)
    scores = jax.nn.sigmoid(logits)
    choice = scores + br_ref[...]
    lane = lax.broadcasted_iota(jnp.int32, (tm, n_exp), 1)
    grp = lane // (n_exp // N_GROUPS)
    neg = -jnp.inf

    def first_argmax(vals):
        m = jnp.max(vals, axis=1, keepdims=True)
        idx = jnp.min(jnp.where(vals == m, lane, n_exp), axis=1, keepdims=True)
        return m, idx

    gscore = []
    for g in range(N_GROUPS):
        vals = jnp.where(grp == g, choice, neg)
        m1, i1 = first_argmax(vals)
        m2 = jnp.max(jnp.where(lane == i1, neg, vals), axis=1, keepdims=True)
        gscore.append(m1 + m2)
    emask = jnp.zeros((tm, n_exp), jnp.bool_)
    for g in range(N_GROUPS):
        beaten = jnp.zeros((tm, 1), jnp.int32)
        for g2 in range(N_GROUPS):
            if g2 == g:
                continue
            wins = (gscore[g2] >= gscore[g]) if g2 < g else (gscore[g2] > gscore[g])
            beaten = beaten + wins.astype(jnp.int32)
        emask = jnp.logical_or(emask, jnp.logical_and(grp == g, beaten < TOPK_GROUPS))

    masked = jnp.where(emask, choice, neg)
    hits = []
    sel = jnp.zeros((tm, n_exp), jnp.bool_)
    for _ in range(TOP_K):
        _, idx = first_argmax(masked)
        hit = lane == idx
        hits.append((idx, hit))
        masked = jnp.where(hit, neg, masked)
        sel = jnp.logical_or(sel, hit)

    wsel = jnp.where(sel, scores, 0.0)
    gate_w = wsel / (jnp.sum(wsel, axis=1, keepdims=True) + 1e-20) * ROUTED_SCALE

    sel_f = jnp.where(sel, 1.0, 0.0)
    r = lax.broadcasted_iota(jnp.int32, (tm, tm), 0)
    c = lax.broadcasted_iota(jnp.int32, (tm, tm), 1)
    before = jnp.where(r > c, 1.0, 0.0).astype(BF16)
    pos = jnp.dot(before, sel_f.astype(BF16), preferred_element_type=F32) + cnt_scr[0:1, :]
    cnt_scr[...] = cnt_scr[...] + jnp.sum(sel_f, axis=0, keepdims=True)

    out_lane = lax.broadcasted_iota(jnp.int32, (tm, LANES), 1)
    e_out = jnp.zeros((tm, LANES), jnp.int32)
    p_out = jnp.zeros((tm, LANES), F32)
    w_out = jnp.zeros((tm, LANES), F32)
    for kk, (idx, hit) in enumerate(hits):
        e_out = jnp.where(out_lane == kk, idx, e_out)
        p_out = jnp.where(out_lane == kk, jnp.sum(jnp.where(hit, pos, 0.0), axis=1, keepdims=True), p_out)
        w_out = jnp.where(out_lane == kk, jnp.sum(jnp.where(hit, gate_w, 0.0), axis=1, keepdims=True), w_out)
    e_ref[...] = e_out
    p_ref[...] = p_out.astype(jnp.int32)
    w_ref[...] = w_out
    cnt_ref[...] = cnt_scr[...]


def _route(h2g, w_router, b_router):
    n, s, _ = h2g.shape
    d, n_exp = w_router.shape
    tm = _tile(n, 512)
    tok = pl.BlockSpec((tm, LANES), lambda i: (i, 0))
    return pl.pallas_call(
        functools.partial(_route_kernel, n_exp=n_exp),
        grid=(n // tm,),
        in_specs=[pl.BlockSpec((tm, s, LANES), lambda i: (i, 0, 0)),
                  pl.BlockSpec((d, n_exp), lambda i: (0, 0)),
                  pl.BlockSpec((1, n_exp), lambda i: (0, 0))],
        out_specs=[tok, tok, tok, pl.BlockSpec((SUBLANES, n_exp), lambda i: (0, 0))],
        out_shape=[jax.ShapeDtypeStruct((n, LANES), jnp.int32),
                   jax.ShapeDtypeStruct((n, LANES), jnp.int32),
                   jax.ShapeDtypeStruct((n, LANES), F32),
                   jax.ShapeDtypeStruct((SUBLANES, n_exp), F32)],
        scratch_shapes=[pltpu.VMEM((SUBLANES, n_exp), F32)],
        compiler_params=_params("arbitrary"),
        name="route",
    )(h2g, w_router, b_router)


def _dest_kernel(e_ref, p_ref, off_ref, d_ref, *, n_exp):
    tm = e_ref.shape[0]
    lane = lax.broadcasted_iota(jnp.int32, (tm, n_exp), 1)
    out_lane = lax.broadcasted_iota(jnp.int32, (tm, LANES), 1)
    e = e_ref[...]
    off = jnp.zeros((tm, LANES), F32)
    for kk in range(TOP_K):
        hit = lane == e[:, kk:kk + 1]
        off_k = jnp.sum(jnp.where(hit, off_ref[0:1, :], 0.0), axis=1, keepdims=True)
        off = jnp.where(out_lane == kk, off_k, off)
    d_ref[...] = p_ref[...] + off.astype(jnp.int32)


def _dest(e_out, p_out, offs8):
    n = e_out.shape[0]
    n_exp = offs8.shape[1]
    tm = _tile(n, 512)
    tok = pl.BlockSpec((tm, LANES), lambda i: (i, 0))
    return pl.pallas_call(
        functools.partial(_dest_kernel, n_exp=n_exp),
        grid=(n // tm,),
        in_specs=[tok, tok, pl.BlockSpec((SUBLANES, n_exp), lambda i: (0, 0))],
        out_specs=tok,
        out_shape=jax.ShapeDtypeStruct((n, LANES), jnp.int32),
        compiler_params=_params("parallel"),
        name="dest",
    )(e_out, p_out, offs8)


def _pad_bits():
    bits, b = [], MOE_SUB // 2
    while b >= 1:
        bits.append(b)
        b //= 2
    return bits


def _dispatch_kernel(pstart_ref, plen_ref, dest_ref, src_hbm, dst_hbm, zbuf, sem, zsem, *, tb, n_exp):
    i = pl.program_id(0)
    n_assign = tb * TOP_K

    def pad_fill(wait):
        def per_expert(e, carry):
            st = pstart_ref[e]
            ln = plen_ref[e]
            for bit in _pad_bits():
                @pl.when((ln & bit) != 0)
                def _():
                    cp = pltpu.make_async_copy(zbuf.at[pl.ds(0, bit)],
                                               dst_hbm.at[pl.ds(st + (ln & ~(2 * bit - 1)), bit)], zsem)
                    if wait:
                        cp.wait()
                    else:
                        cp.start()
            return carry
        lax.fori_loop(0, n_exp, per_expert, 0)

    @pl.when(i == 0)
    def _():
        zbuf[...] = jnp.zeros_like(zbuf)
        pad_fill(False)

    def issue(a, carry):
        tok = i * tb + a // TOP_K
        pltpu.make_async_copy(src_hbm.at[tok], dst_hbm.at[dest_ref[a]], sem).start()
        return carry
    lax.fori_loop(0, n_assign, issue, 0)

    def drain(a, carry):
        pltpu.make_async_copy(src_hbm.at[0], dst_hbm.at[0], sem).wait()
        return carry
    lax.fori_loop(0, n_assign, drain, 0)

    @pl.when(i == 0)
    def _():
        pad_fill(True)


def _dispatch(h2g, dest_flat, pad_start, pad_len, n_rows):
    n, s, _ = h2g.shape
    tb = _tile(n, 512, LANES)
    n_exp = pad_start.shape[0]
    return pl.pallas_call(
        functools.partial(_dispatch_kernel, tb=tb, n_exp=n_exp),
        grid_spec=pltpu.PrefetchScalarGridSpec(
            num_scalar_prefetch=2,
            grid=(n // tb,),
            in_specs=[pl.BlockSpec((tb * TOP_K,), lambda i, *_: (i,), memory_space=pltpu.SMEM),
                      pl.BlockSpec(memory_space=pl.ANY)],
            out_specs=pl.BlockSpec(memory_space=pl.ANY),
            scratch_shapes=[pltpu.VMEM((MOE_SUB // 2, s, LANES), F32),
                            pltpu.SemaphoreType.DMA(()),
                            pltpu.SemaphoreType.DMA(())]),
        out_shape=jax.ShapeDtypeStruct((n_rows, s, LANES), F32),
        compiler_params=_params("arbitrary"),
        name="dispatch",
    )(pad_start, pad_len, dest_flat, h2g)


def _experts_kernel(ie_ref, row0_ref, nsub_ref, xs_hbm, wg_ref, wu_ref, wd_ref, out_hbm,
                    raw, xb, a_scr, stage, xsem, osem, *, f1, f2, n_items):
    del ie_ref
    i = pl.program_id(0)
    s = pl.program_id(1)
    ns = nsub_ref[i]
    spi = MOE_SUBS_PER_ITEM
    cpc = stage.shape[2]

    def x_copy(item, sb):
        return pltpu.make_async_copy(
            xs_hbm.at[pl.ds(row0_ref[item] + sb * MOE_SUB, MOE_SUB)],
            raw.at[pl.ds(sb * MOE_SUB, MOE_SUB)], xsem)

    def x_fetch(item, wait):
        for sb in range(spi):
            @pl.when(sb < nsub_ref[item])
            def _():
                if wait:
                    x_copy(item, sb).wait()
                else:
                    x_copy(item, sb).start()

    def o_copy(item, step, slot, sb):
        return pltpu.make_async_copy(
            stage.at[slot, pl.ds(sb * MOE_SUB, MOE_SUB)],
            out_hbm.at[pl.ds(row0_ref[item] + sb * MOE_SUB, MOE_SUB),
                       pl.ds(_aligned(step * cpc, cpc), cpc)],
            osem.at[slot])

    def o_wait(gstep):
        item = gstep // f2
        for sb in range(spi):
            @pl.when(sb < nsub_ref[item])
            def _():
                o_copy(item, gstep % f2, gstep % 2, sb).wait()

    @pl.when(jnp.logical_and(i == 0, s == 0))
    def _():
        x_fetch(0, False)

    @pl.when(s == 0)
    def _():
        x_fetch(i, True)
        for sb in range(spi):
            @pl.when(sb < ns)
            def _():
                rows = slice(sb * MOE_SUB, (sb + 1) * MOE_SUB)
                for c in range(raw.shape[1]):
                    xb[rows, c * LANES:(c + 1) * LANES] = raw[rows, c, :].astype(BF16)

        @pl.when(i + 1 < n_items)
        def _():
            x_fetch(i + 1, False)

    for nv in range(1, spi + 1):
        m = nv * MOE_SUB

        @pl.when(jnp.logical_and(ns == nv, s < f1))
        def _():
            x = xb[0:m, :]
            g = jnp.dot(x, wg_ref[...].astype(BF16), preferred_element_type=F32)
            u = jnp.dot(x, wu_ref[...].astype(BF16), preferred_element_type=F32)
            a_scr[s, 0:m, :] = (_silu(g) * u).astype(BF16)

    gstep = i * f2 + (s - f1)

    @pl.when(jnp.logical_and(s >= f1, gstep >= 2))
    def _():
        o_wait(gstep - 2)

    for nv in range(1, spi + 1):
        m = nv * MOE_SUB

        @pl.when(jnp.logical_and(ns == nv, s >= f1))
        def _():
            tf = a_scr.shape[2]
            oc = jnp.dot(a_scr[0, 0:m, :], wd_ref[0:tf, :].astype(BF16), preferred_element_type=F32)
            for f in range(1, f1):
                oc = oc + jnp.dot(a_scr[f, 0:m, :], wd_ref[f * tf:(f + 1) * tf, :].astype(BF16),
                                  preferred_element_type=F32)
            slot = gstep % 2
            for c in range(cpc):
                stage[slot, 0:m, c, :] = oc[:, c * LANES:(c + 1) * LANES]
            for sb in range(nv):
                o_copy(i, s - f1, slot, sb).start()

    @pl.when(jnp.logical_and(i == n_items - 1, s == f1 + f2 - 1))
    def _():
        total = n_items * f2
        if total >= 2:
            o_wait(total - 2)
        o_wait(total - 1)


def _experts(xs, item_e, item_row0, item_nsub, w_gate, w_up, w_down):
    n_rows, s_dim, _ = xs.shape
    _, n_exp, d, de = w_gate.shape
    n_items = item_e.shape[0]
    tf = _tile(de, 256, LANES)
    tn = _tile(d, 1024, LANES)
    f1, f2 = de // tf, d // tn
    tm = MOE_SUB * MOE_SUBS_PER_ITEM

    def wg_map(i, s, ie, r0, nsb):
        return (0, ie[i], 0, jnp.minimum(s, f1 - 1))

    def wd_map(i, s, ie, r0, nsb):
        return (0, ie[i], 0, jnp.clip(s - f1, 0, f2 - 1))

    return pl.pallas_call(
        functools.partial(_experts_kernel, f1=f1, f2=f2, n_items=n_items),
        grid_spec=pltpu.PrefetchScalarGridSpec(
            num_scalar_prefetch=3,
            grid=(n_items, f1 + f2),
            in_specs=[pl.BlockSpec(memory_space=pl.ANY),
                      pl.BlockSpec((None, None, d, tf), wg_map),
                      pl.BlockSpec((None, None, d, tf), wg_map),
                      pl.BlockSpec((None, None, de, tn), wd_map)],
            out_specs=pl.BlockSpec(memory_space=pl.ANY),
            scratch_shapes=[pltpu.VMEM((tm, s_dim, LANES), F32),
                            pltpu.VMEM((tm, d), BF16),
                            pltpu.VMEM((f1, tm, tf), BF16),
                            pltpu.VMEM((2, tm, tn // LANES, LANES), F32),
                            pltpu.SemaphoreType.DMA(()),
                            pltpu.SemaphoreType.DMA((2,))]),
        out_shape=jax.ShapeDtypeStruct((n_rows, s_dim, LANES), F32),
        compiler_params=_params("arbitrary", "arbitrary"),
        name="experts",
    )(item_e, item_row0, item_nsub, xs, w_gate, w_up, w_down)


def _shared_kernel(h_ref, wg_ref, wu_ref, wd_ref, o_ref, xb, a_scr, *, f1):
    s = pl.program_id(1)

    @pl.when(s == 0)
    def _():
        xb[...] = _rows_to_matrix(h_ref).astype(BF16)

    @pl.when(s < f1)
    def _():
        x = xb[...]
        g = jnp.dot(x, wg_ref[...], preferred_element_type=F32)
        u = jnp.dot(x, wu_ref[...], preferred_element_type=F32)
        a_scr[s] = (_silu(g) * u).astype(BF16)

    @pl.when(s >= f1)
    def _():
        tf = a_scr.shape[2]
        oc = jnp.dot(a_scr[0], wd_ref[0:tf, :], preferred_element_type=F32)
        for f in range(1, f1):
            oc = oc + jnp.dot(a_scr[f], wd_ref[f * tf:(f + 1) * tf, :], preferred_element_type=F32)
        o_ref[...] = oc


def _shared(h2g, wg, wu, wd):
    n, s_dim, _ = h2g.shape
    d, de = wg.shape
    tm = _tile(n, 512)
    tf = _tile(de, 256, LANES)
    tn = _tile(d, 1024, LANES)
    f1, f2 = de // tf, d // tn
    return pl.pallas_call(
        functools.partial(_shared_kernel, f1=f1),
        grid=(n // tm, f1 + f2),
        in_specs=[pl.BlockSpec((tm, s_dim, LANES), lambda i, s: (i, 0, 0)),
                  pl.BlockSpec((d, tf), lambda i, s: (0, jnp.minimum(s, f1 - 1))),
                  pl.BlockSpec((d, tf), lambda i, s: (0, jnp.minimum(s, f1 - 1))),
                  pl.BlockSpec((de, tn), lambda i, s: (0, jnp.clip(s - f1, 0, f2 - 1)))],
        out_specs=pl.BlockSpec((tm, tn), lambda i, s: (i, jnp.clip(s - f1, 0, f2 - 1))),
        out_shape=jax.ShapeDtypeStruct((n, d), F32),
        scratch_shapes=[pltpu.VMEM((tm, d), BF16), pltpu.VMEM((f1, tm, tf), BF16)],
        compiler_params=_params("parallel", "arbitrary"),
        name="shared",
    )(h2g, wg, wu, wd)


def _combine_kernel(dest_ref, wrep_ref, sh_ref, x1_ref, gate_ref, post_ref, outs_hbm, y_ref,
                    buf, acc_scr, sem, *, tb):
    n_assign = tb * TOP_K

    def issue(a, carry):
        pltpu.make_async_copy(outs_hbm.at[dest_ref[a]], buf.at[a % TOP_K, a // TOP_K], sem).start()
        return carry
    lax.fori_loop(0, n_assign, issue, 0)

    def drain(a, carry):
        pltpu.make_async_copy(outs_hbm.at[0], buf.at[0, 0], sem).wait()
        return carry
    lax.fori_loop(0, n_assign, drain, 0)

    acc = buf[0] * wrep_ref[:, 0:1, :]
    for kk in range(1, TOP_K):
        acc = acc + buf[kk] * wrep_ref[:, kk:kk + 1, :]
    acc_scr[...] = acc
    f = sh_ref[...] + _rows_to_matrix(acc_scr)
    f = f.reshape(x1_ref.shape)
    y_ref[...] = x1_ref[...] + gate_ref[...] * (_rms(f) * post_ref[...])


def _combine(outs, dest_flat, wrep, shared, x1_u, mod_u, post):
    nu, unit, d = x1_u.shape
    s_dim = outs.shape[1]
    tu = _tile(nu, 4, 1)
    tb = tu * unit
    return pl.pallas_call(
        functools.partial(_combine_kernel, tb=tb),
        grid=(nu // tu,),
        in_specs=[pl.BlockSpec((tb * TOP_K,), lambda i: (i,), memory_space=pltpu.SMEM),
                  pl.BlockSpec((tb, TOP_K, LANES), lambda i: (i, 0, 0)),
                  pl.BlockSpec((tb, d), lambda i: (i, 0)),
                  pl.BlockSpec((tu, unit, d), lambda i: (i, 0, 0)),
                  pl.BlockSpec((None, tu, 1, d), lambda i: (5, i, 0, 0)),
                  pl.BlockSpec((1, d), lambda i: (0, 0)),
                  pl.BlockSpec(memory_space=pl.ANY)],
        out_specs=pl.BlockSpec((tu, unit, d), lambda i: (i, 0, 0)),
        out_shape=jax.ShapeDtypeStruct((nu, unit, d), F32),
        scratch_shapes=[pltpu.VMEM((TOP_K, tb, s_dim, LANES), F32),
                        pltpu.VMEM((tb, s_dim, LANES), F32),
                        pltpu.SemaphoreType.DMA(())],
        compiler_params=_params("arbitrary"),
        name="combine",
    )(dest_flat, wrep, shared, x1_u, mod_u, post, outs)


def _work_items(counts, n_assign):
    n_exp = counts.shape[0]
    spi = MOE_SUBS_PER_ITEM
    subs = (counts + MOE_SUB - 1) // MOE_SUB
    padded = subs * MOE_SUB
    offs = jnp.cumsum(padded) - padded
    n_items = n_exp + n_assign // (MOE_SUB * spi)
    items_e = (subs + spi - 1) // spi
    item_end = jnp.cumsum(items_e)
    idx = jnp.arange(n_items, dtype=jnp.int32)
    total = item_end[-1]
    last_e = jnp.max(jnp.where(counts > 0, jnp.arange(n_exp, dtype=jnp.int32), 0))
    e_of = jnp.minimum(jnp.searchsorted(item_end, idx, side="right").astype(jnp.int32), n_exp - 1)
    j = idx - (item_end - items_e)[e_of]
    valid = idx < total
    item_e = jnp.where(valid, e_of, last_e).astype(jnp.int32)
    item_row0 = jnp.where(valid, offs[e_of] + j * (MOE_SUB * spi), 0).astype(jnp.int32)
    item_nsub = jnp.where(valid, jnp.minimum(spi, subs[e_of] - j * spi), 0).astype(jnp.int32)
    n_rows = (n_assign // MOE_SUB + n_exp) * MOE_SUB
    return offs, padded, item_e, item_row0, item_nsub, n_rows


def kernel(x_prompt, x_sample, c_prompt, c_sample, state_gla, state_conv, w_ada, b_ada, norm_mix_pre, norm_mix_post, norm_ffn_pre, norm_ffn_post, w_in, w_decay_up, b_decay_up, gla_norm, conv_w, w_out, w_router, b_router, w_shared_gate, w_shared_up, w_shared_down, w_expert_gate, w_expert_up, w_expert_down):
    bp, tp, d = x_prompt.shape
    bs, ts, _ = x_sample.shape
    depth, _, heads, dk, dv = state_gla.shape
    assert depth == 1 and tp % ts == 0 and ts % SUBLANES == 0 and d % LANES == 0
    unit = ts
    key_w, val_w = heads * dk, heads * dv
    rank = w_decay_up.shape[1]
    cw = conv_w.shape[-1]
    n_exp = w_router.shape[-1]
    n_p, n_s = bp * tp, bs * ts
    n = n_p + n_s
    nu = n // unit

    x_u = jnp.concatenate([x_prompt.reshape(n_p, d), x_sample.reshape(n_s, d)], axis=0).reshape(nu, unit, d)
    nb = bp + bs
    nb8 = -(-nb // SUBLANES) * SUBLANES
    c_all = jnp.pad(jnp.concatenate([c_prompt, c_sample], axis=0), ((0, nb8 - nb), (0, 0)))
    mod = _ada(c_all, w_ada[0], b_ada)
    unit_batch = jnp.concatenate([jnp.repeat(jnp.arange(bp), tp // unit), bp + jnp.arange(bs)])
    mod_u = mod.reshape(nb8, 6, d)[unit_batch].transpose(1, 0, 2)[:, :, None, :]

    q_end = 2 * key_w + 2 * val_w
    w_in0 = w_in[0]
    w_main = jnp.concatenate([w_in0[:, :q_end], w_in0[:, q_end + rank:]], axis=1).astype(BF16)
    w_a = jnp.pad(w_in0[:, q_end:q_end + rank], ((0, 0), (0, LANES - rank))).astype(BF16)
    proj, a_lr = _inproj(x_u, mod_u, norm_mix_pre, w_main, w_a)
    w_up = jnp.pad(w_decay_up[0], ((0, LANES - rank), (0, 0)))
    gla_kw = dict(heads=heads, dk=dk, dv=dv)
    mix = jnp.zeros((n, val_w + cw), BF16)
    mix, gla_p = _gla(proj, a_lr, w_up, b_decay_up, gla_norm, jnp.zeros((bp, heads, dk, dv), F32), mix,
                      row0=0, nb=bp, t_len=tp, **gla_kw)
    mix, gla_s = _gla(proj, a_lr, w_up, b_decay_up, gla_norm, state_gla[0], mix,
                      row0=n_p, nb=bs, t_len=ts, **gla_kw)
    conv_w8 = jnp.pad(conv_w[0], ((0, SUBLANES - conv_w.shape[1]), (0, 0)))
    tail = SUBLANES - state_conv.shape[2]
    conv_kw = dict(col0=q_end, width=cw, val_w=val_w)
    mix, conv_p = _conv(proj, conv_w8, jnp.zeros((bp, SUBLANES, cw), F32), mix,
                        row0=0, nb=bp, t_len=tp, **conv_kw)
    mix, conv_s = _conv(proj, conv_w8, jnp.pad(state_conv[0], ((0, 0), (tail, 0), (0, 0))), mix,
                        row0=n_p, nb=bs, t_len=ts, **conv_kw)
    x1_u, h2g = _outproj(mix, w_out[0].astype(BF16), x_u, mod_u, norm_mix_post, norm_ffn_pre)

    e_out, p_out, w_sel, counts8 = _route(h2g, w_router[0], b_router)
    counts = counts8[0].astype(jnp.int32)
    offs, padded, item_e, item_row0, item_nsub, n_rows = _work_items(counts, n * TOP_K)
    offs8 = jnp.broadcast_to(offs.astype(F32)[None, :], (SUBLANES, n_exp))
    dest_flat = _dest(e_out, p_out, offs8)[:, :TOP_K].reshape(n * TOP_K)
    xs = _dispatch(h2g, dest_flat, (offs + counts).astype(jnp.int32), (padded - counts).astype(jnp.int32),
                   n_rows)
    outs = _experts(xs, item_e, item_row0, item_nsub, w_expert_gate, w_expert_up, w_expert_down)
    shared = _shared(h2g, w_shared_gate[0].astype(BF16), w_shared_up[0].astype(BF16),
                     w_shared_down[0].astype(BF16))
    wrep = jnp.broadcast_to(w_sel[:, :TOP_K, None], (n, TOP_K, LANES))
    y_u = _combine(outs, dest_flat, wrep, shared, x1_u, mod_u, norm_ffn_post)

    y = y_u.reshape(n, d)
    return (y[:n_p].reshape(bp, tp, d), y[n_p:].reshape(bs, ts, d),
            gla_p[None], conv_p[None, :, tail:, :], gla_s[None], conv_s[None, :, tail:, :])
```

```python
import functools

import jax
import jax.numpy as jnp
from jax import lax
from jax.experimental import pallas as pl
from jax.experimental.pallas import tpu as pltpu

CHUNK = 64
TOP_K = 8
N_GROUPS = 8
TOPK_GROUPS = 4
ROUTED_SCALE = 2.5
EPS = 1e-6
GATE_NORMALIZER = 16.0

LANES = 128
SUBLANES = 8
VMEM_LIMIT_BYTES = 56 * 1024 * 1024
SMEM_BLOCK = 1024

MOE_GRAN = 128
MOE_MAX_GRAN = 8
MOE_PAIR = 512

F32 = jnp.float32
BF16 = jnp.bfloat16
U32 = jnp.uint32
HIGHEST = lax.Precision.HIGHEST
NT_DIMS = (((1,), (1,)), ((), ()))
TN_DIMS = (((0,), (0,)), ((), ()))


def _tile(n, pref, mult=SUBLANES):
    t = min(pref, n)
    t -= t % mult
    while t >= mult:
        if n % t == 0:
            return t
        t -= mult
    return n


def _params(*sem):
    return pltpu.CompilerParams(dimension_semantics=sem, vmem_limit_bytes=VMEM_LIMIT_BYTES)


def _silu(x):
    return x * jax.nn.sigmoid(x)


def _rms(x):
    return x * lax.rsqrt(jnp.mean(x * x, axis=-1, keepdims=True) + EPS)


def _bf16_pieces(x, n):
    out = []
    for _ in range(n - 1):
        p = x.astype(BF16).astype(F32)
        out.append(p)
        x = x - p
    out.append(x)
    return out


def _pack_pairs(x, p):
    lo = lax.bitcast_convert_type(x[:, :p].astype(BF16).astype(F32), U32) >> 16
    hi = lax.bitcast_convert_type(x[:, p:].astype(BF16).astype(F32), U32)
    return hi | lo


def _unpack_pairs(w):
    lo = lax.bitcast_convert_type(w << 16, F32)
    hi = lax.bitcast_convert_type(w & jnp.uint32(0xFFFF0000), F32)
    return jnp.concatenate([lo, hi], axis=-1).astype(BF16)


def _pack_row(x, p):
    return jnp.concatenate([_pack_pairs(x[:, b * 2 * p:(b + 1) * 2 * p], p)
                            for b in range(x.shape[1] // (2 * p))], axis=-1)


def _unpack_row(w, p):
    return jnp.concatenate([_unpack_pairs(w[:, b * p:(b + 1) * p]) for b in range(w.shape[1] // p)], axis=-1)


def _ada_kernel(c_ref, w_ref, b_ref, o_ref):
    s = _silu(c_ref[...]).astype(BF16)
    o_ref[...] = jnp.dot(s, w_ref[...].astype(BF16), preferred_element_type=F32) + b_ref[...]


def _ada(c_all, w_ada, b_ada):
    rows, d = c_all.shape
    n = w_ada.shape[1]
    tn = _tile(n, 512, LANES)
    return pl.pallas_call(
        _ada_kernel,
        grid=(n // tn,),
        in_specs=[pl.BlockSpec((rows, d), lambda j: (0, 0)),
                  pl.BlockSpec((d, tn), lambda j: (0, j)),
                  pl.BlockSpec((1, tn), lambda j: (0, j))],
        out_specs=pl.BlockSpec((rows, tn), lambda j: (0, j)),
        out_shape=jax.ShapeDtypeStruct((rows, n), F32),
        compiler_params=_params("arbitrary"),
        name="ada",
    )(c_all, w_ada, b_ada)


def _inproj_kernel(x_ref, sc_ref, sh_ref, g_ref, w_ref, wa_ref, p_ref, a_ref, h_scr):
    @pl.when(pl.program_id(1) == 0)
    def _():
        h = _rms(x_ref[...]) * g_ref[...] * (1.0 + sc_ref[...]) + sh_ref[...]
        hb = h.reshape(h_scr.shape).astype(BF16)
        h_scr[...] = hb
        a_ref[...] = jnp.dot(hb, wa_ref[...], preferred_element_type=F32)

    p_ref[...] = jnp.dot(h_scr[...], w_ref[...], preferred_element_type=F32).astype(p_ref.dtype)


def _inproj(x_u, mod_u, gain, w_main, w_a):
    nu, unit, d = x_u.shape
    npj = w_main.shape[1]
    tu = _tile(nu, 16, 1)
    tm = tu * unit
    tn = _tile(npj, 1024, LANES)
    return pl.pallas_call(
        _inproj_kernel,
        grid=(nu // tu, npj // tn),
        in_specs=[pl.BlockSpec((tu, unit, d), lambda i, j: (i, 0, 0)),
                  pl.BlockSpec((None, tu, 1, d), lambda i, j: (1, i, 0, 0)),
                  pl.BlockSpec((None, tu, 1, d), lambda i, j: (0, i, 0, 0)),
                  pl.BlockSpec((1, d), lambda i, j: (0, 0)),
                  pl.BlockSpec((d, tn), lambda i, j: (0, j)),
                  pl.BlockSpec((d, LANES), lambda i, j: (0, 0))],
        out_specs=[pl.BlockSpec((tm, tn), lambda i, j: (i, j)),
                   pl.BlockSpec((tm, LANES), lambda i, j: (i, 0))],
        out_shape=[jax.ShapeDtypeStruct((nu * unit, npj), BF16),
                   jax.ShapeDtypeStruct((nu * unit, LANES), F32)],
        scratch_shapes=[pltpu.VMEM((tm, d), BF16)],
        compiler_params=_params("parallel", "arbitrary"),
        name="inproj",
    )(x_u, mod_u, mod_u, gain, w_main, w_a)


def _gla_kernel(q_ref, k_ref, v_ref, r_ref, a_ref, wuh_ref, wul_ref, bup_ref, gn_ref, s0_ref, mix_ref,
                o_ref, sout_ref, st_scr, *, chunk, nchunk, dk):
    del mix_ref
    t_id = pl.program_id(2)

    @pl.when(t_id == 0)
    def _():
        st_scr[...] = s0_ref[0, 0].T

    nlev = chunk.bit_length() - 1
    row = lax.broadcasted_iota(jnp.int32, (chunk, chunk), 0)
    col = lax.broadcasted_iota(jnp.int32, (chunk, chunk), 1)
    tri = jnp.where(row >= col, 1.0, 0.0).astype(BF16)
    pick = jnp.concatenate(
        [jnp.where(col == ((row >> (lev + 1)) << (lev + 1)) + (1 << lev) - 1, 1.0, 0.0)
         for lev in range(nlev)], axis=0).astype(BF16)
    trow = lax.broadcasted_iota(jnp.int32, (chunk, dk), 0)
    qscale = dk ** -0.5

    def exact_rows(sel, x):
        parts = jnp.concatenate([p.astype(BF16) for p in _bf16_pieces(x, 3)], axis=-1)
        y = jnp.dot(sel, parts, preferred_element_type=F32)
        w = x.shape[1]
        return y[:, :w] + y[:, w:2 * w] + y[:, 2 * w:]

    def body(c, carry):
        sl = pl.ds(pl.multiple_of(c * chunk, chunk), chunk)
        a_h, a_l = [p.astype(BF16) for p in _bf16_pieces(a_ref[sl, :], 2)]
        wuh = wuh_ref[...]
        gp = (jnp.dot(a_h, wuh, preferred_element_type=F32) + jnp.dot(a_l, wuh, preferred_element_type=F32)
              + jnp.dot(a_h, wul_ref[...], preferred_element_type=F32) + bup_ref[...])
        g = (jnp.minimum(gp, 0.0) - jnp.log(1.0 + jnp.exp(-jnp.abs(gp)))) / GATE_NORMALIZER
        bc = exact_rows(tri, g)
        piv = exact_rows(pick, bc)
        q = q_ref[sl, :].astype(F32) * qscale
        k = k_ref[sl, :].astype(F32)
        v = v_ref[sl, :]
        att = jnp.where(row == col, lax.dot_general(q.astype(BF16), k.astype(BF16), NT_DIMS,
                                                    preferred_element_type=F32), 0.0)
        for lev in range(nlev):
            cpiv = piv[lev * chunk:(lev + 1) * chunk, :]
            upper = ((trow >> lev) & 1) == 1
            qe = jnp.where(upper, q * jnp.exp(jnp.minimum(bc - cpiv, 0.0)), 0.0)
            ke = jnp.where(upper, 0.0, k * jnp.exp(jnp.minimum(cpiv - bc, 0.0)))
            a_lev = lax.dot_general(qe.astype(BF16), ke.astype(BF16), NT_DIMS, preferred_element_type=F32)
            att = att + jnp.where((row >> (lev + 1)) == (col >> (lev + 1)), a_lev, 0.0)
        bl = bc[chunk - 1:chunk, :]
        st = st_scr[...]
        o = jnp.dot(att.astype(BF16), v, preferred_element_type=F32)
        o = o + lax.dot_general((q * jnp.exp(bc)).astype(BF16), st.astype(BF16), NT_DIMS,
                                preferred_element_type=F32)
        upd = lax.dot_general(v, (k * jnp.exp(bl - bc)).astype(BF16), TN_DIMS, preferred_element_type=F32)
        st_scr[...] = st * jnp.exp(bl) + upd
        r = r_ref[sl, :].astype(F32)
        o_ref[sl, :] = (_rms(o) * gn_ref[...] * _silu(r)).astype(o_ref.dtype)
        return carry

    lax.fori_loop(0, nchunk, body, 0)

    @pl.when(t_id == pl.num_programs(2) - 1)
    def _():
        sout_ref[0, 0] = st_scr[...].T


def _gla(proj, a_lr, w_up_hi, w_up_lo, b_up, gn, s0, mix, *, row0, nb, t_len, heads, dk, dv):
    chunk = min(CHUNK, t_len)
    assert chunk & (chunk - 1) == 0
    tt = _tile(t_len, 8 * chunk, chunk)
    nt = t_len // tt
    rb0 = row0 // tt
    kq, kk, kv, kr = 0, heads, (2 * heads * dk) // dv, (2 * heads * dk) // dv + heads

    def rows(b, h, t):
        return rb0 + b * nt + t

    in_specs = [pl.BlockSpec((tt, dk), lambda b, h, t: (rows(b, h, t), kq + h)),
                pl.BlockSpec((tt, dk), lambda b, h, t: (rows(b, h, t), kk + h)),
                pl.BlockSpec((tt, dv), lambda b, h, t: (rows(b, h, t), kv + h)),
                pl.BlockSpec((tt, dv), lambda b, h, t: (rows(b, h, t), kr + h)),
                pl.BlockSpec((tt, LANES), lambda b, h, t: (rows(b, h, t), 0)),
                pl.BlockSpec((LANES, dk), lambda b, h, t: (0, h)),
                pl.BlockSpec((LANES, dk), lambda b, h, t: (0, h)),
                pl.BlockSpec((1, dk), lambda b, h, t: (0, h)),
                pl.BlockSpec((1, dv), lambda b, h, t: (0, 0)),
                pl.BlockSpec((1, 1, dk, dv), lambda b, h, t: (b, h, 0, 0)),
                pl.BlockSpec(memory_space=pl.ANY)]
    return pl.pallas_call(
        functools.partial(_gla_kernel, chunk=chunk, nchunk=tt // chunk, dk=dk),
        grid=(nb, heads, nt),
        in_specs=in_specs,
        out_specs=[pl.BlockSpec((tt, dv), lambda b, h, t: (rows(b, h, t), h)),
                   pl.BlockSpec((1, 1, dk, dv), lambda b, h, t: (b, h, 0, 0))],
        out_shape=[jax.ShapeDtypeStruct(mix.shape, mix.dtype),
                   jax.ShapeDtypeStruct((nb, heads, dk, dv), F32)],
        scratch_shapes=[pltpu.VMEM((dv, dk), F32)],
        input_output_aliases={10: 0},
        compiler_params=_params("parallel", "parallel", "arbitrary"),
        name="gla",
    )(proj, proj, proj, proj, a_lr, w_up_hi, w_up_lo, b_up, gn, s0, mix)


def _conv_kernel(hc_ref, cg_ref, bg_ref, w_ref, c0_ref, mix_ref, y_ref, st_ref, carry_scr):
    del mix_ref
    t = pl.program_id(2)

    @pl.when(t == 0)
    def _():
        carry_scr[...] = c0_ref[0]

    u = cg_ref[...].astype(F32) * hc_ref[...].astype(F32)
    tt = u.shape[0]
    row = lax.broadcasted_iota(jnp.int32, u.shape, 0)
    prev = carry_scr[...]
    p2 = prev[SUBLANES - 2:SUBLANES - 1, :]
    p1 = prev[SUBLANES - 1:SUBLANES, :]
    u1 = jnp.where(row == 0, p1, pltpu.roll(u, 1, 0))
    u2 = jnp.where(row == 0, p2, jnp.where(row == 1, p1, pltpu.roll(u, 2, 0)))
    w = w_ref[...]
    y = w[0:1, :] * u2 + w[1:2, :] * u1 + w[2:3, :] * u
    y_ref[...] = (bg_ref[...].astype(F32) * y).astype(y_ref.dtype)
    tail = u[tt - SUBLANES:tt, :]
    carry_scr[...] = tail

    @pl.when(t == pl.num_programs(2) - 1)
    def _():
        st_ref[0] = tail


def _conv(proj, conv_w8, c0, mix, *, row0, nb, t_len, col0, width, val_w):
    tt = _tile(t_len, 512)
    tc = _tile(width, 1024, LANES)
    nt = t_len // tt
    rb0 = row0 // tt
    cb0 = col0 // tc
    ncb = width // tc

    def rows(b, t):
        return rb0 + b * nt + t

    return pl.pallas_call(
        _conv_kernel,
        grid=(nb, ncb, nt),
        in_specs=[pl.BlockSpec((tt, tc), lambda b, j, t: (rows(b, t), cb0 + j)),
                  pl.BlockSpec((tt, tc), lambda b, j, t: (rows(b, t), cb0 + ncb + j)),
                  pl.BlockSpec((tt, tc), lambda b, j, t: (rows(b, t), cb0 + 2 * ncb + j)),
                  pl.BlockSpec((SUBLANES, tc), lambda b, j, t: (0, j)),
                  pl.BlockSpec((1, SUBLANES, tc), lambda b, j, t: (b, 0, j)),
                  pl.BlockSpec(memory_space=pl.ANY)],
        out_specs=[pl.BlockSpec((tt, tc), lambda b, j, t: (rows(b, t), val_w // tc + j)),
                   pl.BlockSpec((1, SUBLANES, tc), lambda b, j, t: (b, 0, j))],
        out_shape=[jax.ShapeDtypeStruct(mix.shape, mix.dtype),
                   jax.ShapeDtypeStruct((nb, SUBLANES, width), F32)],
        scratch_shapes=[pltpu.VMEM((SUBLANES, tc), F32)],
        input_output_aliases={5: 0},
        compiler_params=_params("parallel", "parallel", "arbitrary"),
        name="conv",
    )(proj, proj, proj, conv_w8, c0, mix)


def _outproj_kernel(mix_ref, w_ref, x_ref, gate_ref, post_ref, pre2_ref, sc2_ref, sh2_ref,
                    x1_ref, h2_ref, h2p_ref, acc, *, pair):
    kk = pl.program_id(1)

    @pl.when(kk == 0)
    def _():
        acc[...] = jnp.zeros_like(acc)

    acc[...] += jnp.dot(mix_ref[...], w_ref[...], preferred_element_type=F32)

    @pl.when(kk == pl.num_programs(1) - 1)
    def _():
        out = acc[...].reshape(x_ref.shape)
        x1 = x_ref[...] + gate_ref[...] * (_rms(out) * post_ref[...])
        x1_ref[...] = x1
        h2 = _rms(x1) * pre2_ref[...] * (1.0 + sc2_ref[...]) + sh2_ref[...]
        h2 = h2.reshape(acc.shape)
        h2_ref[...] = h2
        h2p_ref[...] = _pack_row(h2, pair)


def _outproj(mix, w_out, x_u, mod_u, post, pre2, pair):
    nu, unit, d = x_u.shape
    kdim = mix.shape[1]
    tu = _tile(nu, 8, 1)
    tm = tu * unit
    tk = _tile(kdim, 1024, LANES)

    def mod(comp):
        return pl.BlockSpec((None, tu, 1, d), lambda i, k: (comp, i, 0, 0))

    vec = pl.BlockSpec((1, d), lambda i, k: (0, 0))
    return pl.pallas_call(
        functools.partial(_outproj_kernel, pair=pair),
        grid=(nu // tu, kdim // tk),
        in_specs=[pl.BlockSpec((tm, tk), lambda i, k: (i, k)),
                  pl.BlockSpec((tk, d), lambda i, k: (k, 0)),
                  pl.BlockSpec((tu, unit, d), lambda i, k: (i, 0, 0)),
                  mod(2), vec, vec, mod(4), mod(3)],
        out_specs=[pl.BlockSpec((tu, unit, d), lambda i, k: (i, 0, 0)),
                   pl.BlockSpec((tm, d), lambda i, k: (i, 0)),
                   pl.BlockSpec((tm, None, d // 2), lambda i, k: (i, 0, 0))],
        out_shape=[jax.ShapeDtypeStruct((nu, unit, d), F32),
                   jax.ShapeDtypeStruct((nu * unit, d), F32),
                   jax.ShapeDtypeStruct((nu * unit, 1, d // 2), U32)],
        scratch_shapes=[pltpu.VMEM((tm, d), F32)],
        compiler_params=_params("parallel", "arbitrary"),
        name="outproj",
    )(mix, w_out, x_u, mod_u, post, pre2, mod_u, mod_u)


def _route_kernel(h_ref, wr_ref, br_ref, e_ref, w_ref, cnt_ref, cnt_scr, *, n_exp):
    i = pl.program_id(0)

    @pl.when(i == 0)
    def _():
        cnt_scr[...] = jnp.zeros_like(cnt_scr)

    tm = h_ref.shape[0]
    logits = jnp.dot(h_ref[...], wr_ref[...], precision=HIGHEST, preferred_element_type=F32)
    scores = jax.nn.sigmoid(logits)
    choice = scores + br_ref[...]
    lane = lax.broadcasted_iota(jnp.int32, (tm, n_exp), 1)
    grp = lane // (n_exp // N_GROUPS)
    neg = -jnp.inf

    def first_argmax(vals):
        m = jnp.max(vals, axis=1, keepdims=True)
        idx = jnp.min(jnp.where(vals == m, lane, n_exp), axis=1, keepdims=True)
        return m, idx

    gscore = []
    for g in range(N_GROUPS):
        vals = jnp.where(grp == g, choice, neg)
        m1, i1 = first_argmax(vals)
        m2 = jnp.max(jnp.where(lane == i1, neg, vals), axis=1, keepdims=True)
        gscore.append(m1 + m2)
    emask = jnp.zeros((tm, n_exp), jnp.bool_)
    for g in range(N_GROUPS):
        beaten = jnp.zeros((tm, 1), jnp.int32)
        for g2 in range(N_GROUPS):
            if g2 == g:
                continue
            wins = (gscore[g2] >= gscore[g]) if g2 < g else (gscore[g2] > gscore[g])
            beaten = beaten + wins.astype(jnp.int32)
        emask = jnp.logical_or(emask, jnp.logical_and(grp == g, beaten < TOPK_GROUPS))

    masked = jnp.where(emask, choice, neg)
    hits = []
    sel = jnp.zeros((tm, n_exp), jnp.bool_)
    for _ in range(TOP_K):
        _, idx = first_argmax(masked)
        hit = lane == idx
        hits.append((idx, hit))
        masked = jnp.where(hit, neg, masked)
        sel = jnp.logical_or(sel, hit)

    wsel = jnp.where(sel, scores, 0.0)
    gate_w = wsel / (jnp.sum(wsel, axis=1, keepdims=True) + 1e-20) * ROUTED_SCALE
    cnt_scr[...] = cnt_scr[...] + jnp.sum(jnp.where(sel, 1.0, 0.0), axis=0, keepdims=True)

    out_lane = lax.broadcasted_iota(jnp.int32, (tm, LANES), 1)
    e_out = jnp.zeros((tm, LANES), jnp.int32)
    w_out = jnp.zeros((tm, LANES), F32)
    for kk, (idx, hit) in enumerate(hits):
        e_out = jnp.where(out_lane == kk, idx, e_out)
        w_out = jnp.where(out_lane == kk, jnp.sum(jnp.where(hit, gate_w, 0.0), axis=1, keepdims=True), w_out)
    e_ref[...] = e_out
    w_ref[...] = w_out
    cnt_ref[...] = cnt_scr[...]


def _route(h2, w_router, b_router):
    n, d = h2.shape
    n_exp = w_router.shape[1]
    tm = _tile(n, 512)
    tok = pl.BlockSpec((tm, LANES), lambda i: (i, 0))
    return pl.pallas_call(
        functools.partial(_route_kernel, n_exp=n_exp),
        grid=(n // tm,),
        in_specs=[pl.BlockSpec((tm, d), lambda i: (i, 0)),
                  pl.BlockSpec((d, n_exp), lambda i: (0, 0)),
                  pl.BlockSpec((1, n_exp), lambda i: (0, 0))],
        out_specs=[tok, tok, pl.BlockSpec((SUBLANES, n_exp), lambda i: (0, 0))],
        out_shape=[jax.ShapeDtypeStruct((n, LANES), jnp.int32),
                   jax.ShapeDtypeStruct((n, LANES), F32),
                   jax.ShapeDtypeStruct((SUBLANES, n_exp), F32)],
        scratch_shapes=[pltpu.VMEM((SUBLANES, n_exp), F32)],
        compiler_params=_params("arbitrary"),
        name="route",
    )(h2, w_router, b_router)


def _experts_kernel(ie_ref, ngr_ref, idx_ref, idxn_ref, h_hbm, wg_ref, wu_ref, wd_ref, y_hbm,
                    raw, xb, a_scr, stage, gsem, ssem, *, f1, f2, n_items, n_tok, pair):
    del ie_ref
    i = pl.program_id(0)
    s = pl.program_id(1)
    ngr = ngr_ref[i]
    tn = wd_ref.shape[1]

    def gather_start(idx, n_gran):
        def body(r8, carry):
            base = pl.multiple_of(r8 * SUBLANES, SUBLANES)
            for j in range(SUBLANES):
                tok = jnp.minimum(idx[base + j] >> 3, n_tok - 1)
                pltpu.make_async_copy(h_hbm.at[tok], raw.at[pl.ds(base + j, 1), :], gsem).start()
            return carry
        lax.fori_loop(0, n_gran * (MOE_GRAN // SUBLANES), body, 0)

    def scatter_start(n_gran):
        def body(r8, carry):
            base = pl.multiple_of(r8 * SUBLANES, SUBLANES)
            for j in range(SUBLANES):
                pltpu.make_async_copy(stage.at[pl.ds(base + j, 1), :], y_hbm.at[idx_ref[base + j]], ssem).start()
            return carry
        lax.fori_loop(0, n_gran * (MOE_GRAN // SUBLANES), body, 0)

    def rows_wait(n_gran, src, dst, sem):
        for g in range(MOE_MAX_GRAN):
            @pl.when(g < n_gran)
            def _():
                for _ in range(MOE_GRAN):
                    pltpu.make_async_copy(src, dst, sem).wait()

    row0 = pl.ds(0, 1)

    @pl.when(jnp.logical_and(i == 0, s == 0))
    def _():
        gather_start(idx_ref, ngr)

    @pl.when(s == 0)
    def _():
        rows_wait(ngr, h_hbm.at[0], raw.at[row0, :], gsem)
        for g in range(MOE_MAX_GRAN):
            @pl.when(g < ngr)
            def _():
                rows = slice(g * MOE_GRAN, (g + 1) * MOE_GRAN)
                xb[rows, :] = _unpack_row(raw[rows, :], pair)

        @pl.when(i + 1 < n_items)
        def _():
            gather_start(idxn_ref, ngr_ref[jnp.minimum(i + 1, n_items - 1)])

    for nv in range(1, MOE_MAX_GRAN + 1):
        m = nv * MOE_GRAN

        @pl.when(jnp.logical_and(ngr == nv, s < f1))
        def _():
            x = xb[0:m, :]
            g = jnp.dot(x, wg_ref[...].astype(BF16), preferred_element_type=F32)
            u = jnp.dot(x, wu_ref[...].astype(BF16), preferred_element_type=F32)
            a_scr[s, 0:m, :] = (_silu(g) * u).astype(BF16)

    @pl.when(jnp.logical_and(s == f1, i > 0))
    def _():
        rows_wait(ngr_ref[jnp.maximum(i - 1, 0)], stage.at[row0, :], y_hbm.at[0], ssem)

    for nv in range(1, MOE_MAX_GRAN + 1):
        m = nv * MOE_GRAN

        @pl.when(jnp.logical_and(ngr == nv, s >= f1))
        def _():
            tf = a_scr.shape[2]
            oc = jnp.dot(a_scr[0, 0:m, :], wd_ref[0:tf, :].astype(BF16), preferred_element_type=F32)
            for f in range(1, f1):
                oc = oc + jnp.dot(a_scr[f, 0:m, :], wd_ref[f * tf:(f + 1) * tf, :].astype(BF16),
                                  preferred_element_type=F32)
            col = pl.multiple_of((s - f1) * (tn // 2), tn // 2)
            stage[0:m, pl.ds(col, tn // 2)] = _pack_row(oc, pair)

    @pl.when(s == f1 + f2 - 1)
    def _():
        scatter_start(ngr)

    @pl.when(jnp.logical_and(i == n_items - 1, s == f1 + f2 - 1))
    def _():
        rows_wait(ngr, stage.at[row0, :], y_hbm.at[0], ssem)


def _experts(h2p, item_e, item_ngr, item_idx, w_gate, w_up, w_down, n_out_rows, pair):
    n_tok, _, half = h2p.shape
    _, n_exp, d, de = w_gate.shape
    n_items = item_e.shape[0]
    tf = _tile(de, 256, LANES)
    tn = 2 * pair
    f1, f2 = de // tf, d // tn
    tm = MOE_GRAN * MOE_MAX_GRAN

    def wg_map(i, s, ie, ng):
        return (0, ie[i], 0, jnp.minimum(s, f1 - 1))

    def wd_map(i, s, ie, ng):
        return (0, ie[i], 0, jnp.clip(s - f1, 0, f2 - 1))

    return pl.pallas_call(
        functools.partial(_experts_kernel, f1=f1, f2=f2, n_items=n_items, n_tok=n_tok, pair=pair),
        grid_spec=pltpu.PrefetchScalarGridSpec(
            num_scalar_prefetch=2,
            grid=(n_items, f1 + f2),
            in_specs=[pl.BlockSpec((SMEM_BLOCK,), lambda i, s, ie, ng: (i,), memory_space=pltpu.SMEM),
                      pl.BlockSpec((SMEM_BLOCK,), lambda i, s, ie, ng: (jnp.minimum(i + 1, n_items - 1),),
                                   memory_space=pltpu.SMEM),
                      pl.BlockSpec(memory_space=pl.ANY),
                      pl.BlockSpec((None, None, d, tf), wg_map),
                      pl.BlockSpec((None, None, d, tf), wg_map),
                      pl.BlockSpec((None, None, de, tn), wd_map)],
            out_specs=pl.BlockSpec(memory_space=pl.ANY),
            scratch_shapes=[pltpu.VMEM((tm, half), U32),
                            pltpu.VMEM((tm, d), BF16),
                            pltpu.VMEM((f1, tm, tf), BF16),
                            pltpu.VMEM((tm, half), U32),
                            pltpu.SemaphoreType.DMA(()),
                            pltpu.SemaphoreType.DMA(())]),
        out_shape=jax.ShapeDtypeStruct((n_out_rows, 1, half), U32),
        compiler_params=_params("arbitrary", "arbitrary"),
        name="experts",
    )(item_e, item_ngr, item_idx, item_idx, h2p, w_gate, w_up, w_down)


def _shared_kernel(h_ref, wg_ref, wu_ref, wd_ref, o_ref, xb, a_scr, *, f1):
    s = pl.program_id(1)

    @pl.when(s == 0)
    def _():
        xb[...] = h_ref[...].astype(BF16)

    @pl.when(s < f1)
    def _():
        x = xb[...]
        g = jnp.dot(x, wg_ref[...], preferred_element_type=F32)
        u = jnp.dot(x, wu_ref[...], preferred_element_type=F32)
        a_scr[s] = (_silu(g) * u).astype(BF16)

    @pl.when(s >= f1)
    def _():
        tf = a_scr.shape[2]
        oc = jnp.dot(a_scr[0], wd_ref[0:tf, :], preferred_element_type=F32)
        for f in range(1, f1):
            oc = oc + jnp.dot(a_scr[f], wd_ref[f * tf:(f + 1) * tf, :], preferred_element_type=F32)
        o_ref[...] = oc


def _shared(h2, wg, wu, wd):
    n, d = h2.shape
    de = wg.shape[1]
    tm = _tile(n, 512)
    tf = _tile(de, 256, LANES)
    tn = _tile(d, 1024, LANES)
    f1, f2 = de // tf, d // tn
    return pl.pallas_call(
        functools.partial(_shared_kernel, f1=f1),
        grid=(n // tm, f1 + f2),
        in_specs=[pl.BlockSpec((tm, d), lambda i, s: (i, 0)),
                  pl.BlockSpec((d, tf), lambda i, s: (0, jnp.minimum(s, f1 - 1))),
                  pl.BlockSpec((d, tf), lambda i, s: (0, jnp.minimum(s, f1 - 1))),
                  pl.BlockSpec((de, tn), lambda i, s: (0, jnp.clip(s - f1, 0, f2 - 1)))],
        out_specs=pl.BlockSpec((tm, tn), lambda i, s: (i, jnp.clip(s - f1, 0, f2 - 1))),
        out_shape=jax.ShapeDtypeStruct((n, d), F32),
        scratch_shapes=[pltpu.VMEM((tm, d), BF16), pltpu.VMEM((f1, tm, tf), BF16)],
        compiler_params=_params("parallel", "arbitrary"),
        name="shared",
    )(h2, wg, wu, wd)


def _combine_kernel(y_ref, wt_ref, sh_ref, x1_ref, gate_ref, post_ref, o_ref, *, pair):
    tb = sh_ref.shape[0]
    na = tb * TOP_K
    r = lax.broadcasted_iota(jnp.int32, (tb, na), 0)
    c = lax.broadcasted_iota(jnp.int32, (tb, na), 1)
    wmat = jnp.where((c >> 3) == r, wt_ref[...], 0.0)
    vals = _unpack_row(y_ref[...], pair)
    routed = jnp.zeros((tb, vals.shape[1]), F32)
    for piece in _bf16_pieces(wmat, 2):
        routed = routed + jnp.dot(piece.astype(BF16), vals, preferred_element_type=F32)
    f = (sh_ref[...] + routed).reshape(x1_ref.shape)
    o_ref[...] = x1_ref[...] + gate_ref[...] * (_rms(f) * post_ref[...])


def _combine(ycomb, wtile, shared, x1_u, mod_u, post, pair):
    nu, unit, d = x1_u.shape
    tu = _tile(nu, 4, 1)
    tb = tu * unit
    assert TOP_K == SUBLANES
    return pl.pallas_call(
        functools.partial(_combine_kernel, pair=pair),
        grid=(nu // tu,),
        in_specs=[pl.BlockSpec((tb * TOP_K, None, d // 2), lambda i: (i, 0, 0)),
                  pl.BlockSpec((tb, tb * TOP_K), lambda i: (i, 0)),
                  pl.BlockSpec((tb, d), lambda i: (i, 0)),
                  pl.BlockSpec((tu, unit, d), lambda i: (i, 0, 0)),
                  pl.BlockSpec((None, tu, 1, d), lambda i: (5, i, 0, 0)),
                  pl.BlockSpec((1, d), lambda i: (0, 0))],
        out_specs=pl.BlockSpec((tu, unit, d), lambda i: (i, 0, 0)),
        out_shape=jax.ShapeDtypeStruct((nu, unit, d), F32),
        compiler_params=_params("parallel"),
        name="combine",
    )(ycomb, wtile, shared, x1_u, mod_u, post)


def _work_items(flat_e, counts, n_assign):
    n_exp = counts.shape[0]
    tm = MOE_GRAN * MOE_MAX_GRAN
    assert n_assign % tm == 0 and tm <= SMEM_BLOCK
    n_items = n_assign // tm + n_exp
    unused = 2 * n_exp + 1
    pad_e = jnp.arange(n_exp, dtype=jnp.int32)[:, None]
    pad_j = jnp.arange(tm, dtype=jnp.int32)[None, :]
    pad_key = jnp.where(pad_j < ((-counts) % tm)[:, None], 2 * pad_e + 1, unused).reshape(n_exp * tm)
    pad_val = jnp.broadcast_to(n_assign + pad_j % MOE_GRAN, (n_exp, tm)).reshape(n_exp * tm)
    keys = jnp.concatenate([2 * flat_e, pad_key])
    vals = jnp.concatenate([jnp.arange(n_assign, dtype=jnp.int32), pad_val])
    keys, vals = lax.sort_key_val(keys, vals)
    keys = keys.reshape(n_items, tm)
    last_e = jnp.max(jnp.where(counts > 0, jnp.arange(n_exp, dtype=jnp.int32), 0))
    item_e = jnp.where(keys[:, 0] < unused, keys[:, 0] >> 1, last_e).astype(jnp.int32)
    cnt = jnp.sum((keys & 1) == 0, axis=1)
    item_ngr = ((cnt + MOE_GRAN - 1) // MOE_GRAN).astype(jnp.int32)
    item_idx = jnp.pad(vals.reshape(n_items, tm), ((0, 0), (0, SMEM_BLOCK - tm)))
    return item_e, item_ngr, item_idx.reshape(n_items * SMEM_BLOCK)


def kernel(x_prompt, x_sample, c_prompt, c_sample, state_gla, state_conv, w_ada, b_ada, norm_mix_pre, norm_mix_post, norm_ffn_pre, norm_ffn_post, w_in, w_decay_up, b_decay_up, gla_norm, conv_w, w_out, w_router, b_router, w_shared_gate, w_shared_up, w_shared_down, w_expert_gate, w_expert_up, w_expert_down):
    bp, tp, d = x_prompt.shape
    bs, ts, _ = x_sample.shape
    depth, _, heads, dk, dv = state_gla.shape
    assert depth == 1 and tp % ts == 0 and ts % SUBLANES == 0 and d % LANES == 0
    unit = ts
    key_w, val_w = heads * dk, heads * dv
    rank = w_decay_up.shape[1]
    cw = conv_w.shape[-1]
    n_p, n_s = bp * tp, bs * ts
    n = n_p + n_s
    nu = n // unit
    pair = min(MOE_PAIR, d // 2)

    x_u = jnp.concatenate([x_prompt.reshape(n_p, d), x_sample.reshape(n_s, d)], axis=0).reshape(nu, unit, d)
    nb = bp + bs
    nb8 = -(-nb // SUBLANES) * SUBLANES
    c_all = jnp.pad(jnp.concatenate([c_prompt, c_sample], axis=0), ((0, nb8 - nb), (0, 0)))
    mod = _ada(c_all, w_ada[0], b_ada)
    unit_batch = jnp.concatenate([jnp.repeat(jnp.arange(bp), tp // unit), bp + jnp.arange(bs)])
    mod_u = mod.reshape(nb8, 6, d)[unit_batch].transpose(1, 0, 2)[:, :, None, :]

    q_end = 2 * key_w + 2 * val_w
    w_in0 = w_in[0]
    w_main = jnp.concatenate([w_in0[:, :q_end], w_in0[:, q_end + rank:]], axis=1).astype(BF16)
    w_a = jnp.pad(w_in0[:, q_end:q_end + rank], ((0, 0), (0, LANES - rank))).astype(BF16)
    proj, a_lr = _inproj(x_u, mod_u, norm_mix_pre, w_main, w_a)
    w_up = jnp.pad(w_decay_up[0], ((0, LANES - rank), (0, 0)))
    w_up_hi = w_up.astype(BF16)
    w_up_lo = (w_up - w_up_hi.astype(F32)).astype(BF16)
    gla_kw = dict(heads=heads, dk=dk, dv=dv)
    mix = jnp.zeros((n, val_w + cw), BF16)
    mix, gla_p = _gla(proj, a_lr, w_up_hi, w_up_lo, b_decay_up, gla_norm, jnp.zeros((bp, heads, dk, dv), F32),
                      mix, row0=0, nb=bp, t_len=tp, **gla_kw)
    mix, gla_s = _gla(proj, a_lr, w_up_hi, w_up_lo, b_decay_up, gla_norm, state_gla[0], mix,
                      row0=n_p, nb=bs, t_len=ts, **gla_kw)
    conv_w8 = jnp.pad(conv_w[0], ((0, SUBLANES - conv_w.shape[1]), (0, 0)))
    tail = SUBLANES - state_conv.shape[2]
    conv_kw = dict(col0=q_end, width=cw, val_w=val_w)
    mix, conv_p = _conv(proj, conv_w8, jnp.zeros((bp, SUBLANES, cw), F32), mix,
                        row0=0, nb=bp, t_len=tp, **conv_kw)
    mix, conv_s = _conv(proj, conv_w8, jnp.pad(state_conv[0], ((0, 0), (tail, 0), (0, 0))), mix,
                        row0=n_p, nb=bs, t_len=ts, **conv_kw)
    x1_u, h2, h2p = _outproj(mix, w_out[0].astype(BF16), x_u, mod_u, norm_mix_post, norm_ffn_pre, pair)

    e_sel, w_sel, counts8 = _route(h2, w_router[0], b_router)
    n_assign = n * TOP_K
    flat_e = e_sel[:, :TOP_K].reshape(n_assign)
    item_e, item_ngr, item_idx = _work_items(flat_e, counts8[0].astype(jnp.int32), n_assign)
    ycomb = _experts(h2p, item_e, item_ngr, item_idx, w_expert_gate, w_expert_up, w_expert_down,
                     n_assign + MOE_GRAN, pair)
    shared = _shared(h2, w_shared_gate[0].astype(BF16), w_shared_up[0].astype(BF16),
                     w_shared_down[0].astype(BF16))
    tb = _tile(nu, 4, 1) * unit
    wtile = jnp.tile(w_sel[:, :TOP_K], (1, tb))
    y_u = _combine(ycomb, wtile, shared, x1_u, mod_u, norm_ffn_post, pair)

    y = y_u.reshape(n, d)
    return (y[:n_p].reshape(bp, tp, d), y[n_p:].reshape(bs, ts, d),
            gla_p[None], conv_p[None, :, tail:, :], gla_s[None], conv_s[None, :, tail:, :])
```

```python
import functools

import jax
import jax.numpy as jnp
from jax import lax
from jax.experimental import pallas as pl
from jax.experimental.pallas import tpu as pltpu

CHUNK = 64
TOP_K = 8
N_GROUPS = 8
TOPK_GROUPS = 4
ROUTED_SCALE = 2.5
EPS = 1e-6
GATE_NORMALIZER = 16.0

LANES = 128
SUBLANES = 8
VMEM_LIMIT_BYTES = 56 * 1024 * 1024
SMEM_BLOCK = 1024

MOE_GRAN = 128
MOE_MAX_GRAN = 8
MOE_PAIR = 512

F32 = jnp.float32
BF16 = jnp.bfloat16
U32 = jnp.uint32
HIGHEST = lax.Precision.HIGHEST
NT_DIMS = (((1,), (1,)), ((), ()))
TN_DIMS = (((0,), (0,)), ((), ()))


def _tile(n, pref, mult=SUBLANES):
    t = min(pref, n)
    t -= t % mult
    while t >= mult:
        if n % t == 0:
            return t
        t -= mult
    return n


def _params(*sem):
    return pltpu.CompilerParams(dimension_semantics=sem, vmem_limit_bytes=VMEM_LIMIT_BYTES)


def _silu(x):
    return x * jax.nn.sigmoid(x)


def _rms(x):
    return x * lax.rsqrt(jnp.mean(x * x, axis=-1, keepdims=True) + EPS)


def _bf16_pieces(x, n):
    out = []
    for _ in range(n - 1):
        p = x.astype(BF16).astype(F32)
        out.append(p)
        x = x - p
    out.append(x)
    return out


def _pack_pairs(x, p):
    lo = lax.bitcast_convert_type(x[:, :p].astype(BF16).astype(F32), U32) >> 16
    hi = lax.bitcast_convert_type(x[:, p:].astype(BF16).astype(F32), U32)
    return hi | lo


def _unpack_pairs(w):
    lo = lax.bitcast_convert_type(w << 16, F32)
    hi = lax.bitcast_convert_type(w & jnp.uint32(0xFFFF0000), F32)
    return jnp.concatenate([lo, hi], axis=-1).astype(BF16)


def _pack_row(x, p):
    return jnp.concatenate([_pack_pairs(x[:, b * 2 * p:(b + 1) * 2 * p], p)
                            for b in range(x.shape[1] // (2 * p))], axis=-1)


def _unpack_row(w, p):
    return jnp.concatenate([_unpack_pairs(w[:, b * p:(b + 1) * p]) for b in range(w.shape[1] // p)], axis=-1)


def _ada_kernel(c_ref, w_ref, b_ref, o_ref):
    s = _silu(c_ref[...]).astype(BF16)
    o_ref[...] = jnp.dot(s, w_ref[...].astype(BF16), preferred_element_type=F32) + b_ref[...]


def _ada(c_all, w_ada, b_ada):
    rows, d = c_all.shape
    n = w_ada.shape[1]
    tn = _tile(n, 512, LANES)
    return pl.pallas_call(
        _ada_kernel,
        grid=(n // tn,),
        in_specs=[pl.BlockSpec((rows, d), lambda j: (0, 0)),
                  pl.BlockSpec((d, tn), lambda j: (0, j)),
                  pl.BlockSpec((1, tn), lambda j: (0, j))],
        out_specs=pl.BlockSpec((rows, tn), lambda j: (0, j)),
        out_shape=jax.ShapeDtypeStruct((rows, n), F32),
        compiler_params=_params("arbitrary"),
        name="ada",
    )(c_all, w_ada, b_ada)


def _inproj_kernel(x_ref, sc_ref, sh_ref, g_ref, w_ref, wa_ref, p_ref, a_ref, h_scr):
    @pl.when(pl.program_id(1) == 0)
    def _():
        h = _rms(x_ref[...]) * g_ref[...] * (1.0 + sc_ref[...]) + sh_ref[...]
        hb = h.reshape(h_scr.shape).astype(BF16)
        h_scr[...] = hb
        a_ref[...] = jnp.dot(hb, wa_ref[...], preferred_element_type=F32)

    p_ref[...] = jnp.dot(h_scr[...], w_ref[...], preferred_element_type=F32).astype(p_ref.dtype)


def _inproj(x_u, mod_u, gain, w_main, w_a):
    nu, unit, d = x_u.shape
    npj = w_main.shape[1]
    tu = _tile(nu, 16, 1)
    tm = tu * unit
    tn = _tile(npj, 1024, LANES)
    return pl.pallas_call(
        _inproj_kernel,
        grid=(nu // tu, npj // tn),
        in_specs=[pl.BlockSpec((tu, unit, d), lambda i, j: (i, 0, 0)),
                  pl.BlockSpec((None, tu, 1, d), lambda i, j: (1, i, 0, 0)),
                  pl.BlockSpec((None, tu, 1, d), lambda i, j: (0, i, 0, 0)),
                  pl.BlockSpec((1, d), lambda i, j: (0, 0)),
                  pl.BlockSpec((d, tn), lambda i, j: (0, j)),
                  pl.BlockSpec((d, LANES), lambda i, j: (0, 0))],
        out_specs=[pl.BlockSpec((tm, tn), lambda i, j: (i, j)),
                   pl.BlockSpec((tm, LANES), lambda i, j: (i, 0))],
        out_shape=[jax.ShapeDtypeStruct((nu * unit, npj), BF16),
                   jax.ShapeDtypeStruct((nu * unit, LANES), F32)],
        scratch_shapes=[pltpu.VMEM((tm, d), BF16)],
        compiler_params=_params("parallel", "arbitrary"),
        name="inproj",
    )(x_u, mod_u, mod_u, gain, w_main, w_a)


def _gla_kernel(q_ref, k_ref, v_ref, r_ref, a_ref, wuh_ref, wul_ref, bup_ref, gn_ref, s0_ref, mix_ref,
                o_ref, sout_ref, st_scr, *, chunk, nchunk, dk):
    del mix_ref
    t_id = pl.program_id(2)

    hb = st_scr.shape[0]
    dv = st_scr.shape[1]

    @pl.when(t_id == 0)
    def _():
        for hh in range(hb):
            st_scr[hh] = s0_ref[0, hh].T

    nlev = chunk.bit_length() - 1
    row = lax.broadcasted_iota(jnp.int32, (chunk, chunk), 0)
    col = lax.broadcasted_iota(jnp.int32, (chunk, chunk), 1)
    tri = jnp.where(row >= col, 1.0, 0.0).astype(BF16)
    pick = jnp.concatenate(
        [jnp.where(col == ((row >> (lev + 1)) << (lev + 1)) + (1 << lev) - 1, 1.0, 0.0)
         for lev in range(nlev)], axis=0).astype(BF16)
    trow = lax.broadcasted_iota(jnp.int32, (chunk, dk), 0)
    qscale = dk ** -0.5

    def exact_rows(sel, x):
        parts = jnp.concatenate([p.astype(BF16) for p in _bf16_pieces(x, 3)], axis=-1)
        y = jnp.dot(sel, parts, preferred_element_type=F32)
        w = x.shape[1]
        return y[:, :w] + y[:, w:2 * w] + y[:, 2 * w:]

    def body(c, carry):
        sl = pl.ds(pl.multiple_of(c * chunk, chunk), chunk)
        a_h, a_l = [p.astype(BF16) for p in _bf16_pieces(a_ref[sl, :], 2)]
        wuh = wuh_ref[...]
        gp = (jnp.dot(a_h, wuh, preferred_element_type=F32) + jnp.dot(a_l, wuh, preferred_element_type=F32)
              + jnp.dot(a_h, wul_ref[...], preferred_element_type=F32) + bup_ref[...])
        g = (jnp.minimum(gp, 0.0) - jnp.log(1.0 + jnp.exp(-jnp.abs(gp)))) / GATE_NORMALIZER
        bc_all = exact_rows(tri, g)
        piv_all = exact_rows(pick, bc_all)
        for hh in range(hb):
            kcols = slice(hh * dk, (hh + 1) * dk)
            vcols = slice(hh * dv, (hh + 1) * dv)
            bc = bc_all[:, kcols]
            q = q_ref[sl, kcols].astype(F32) * qscale
            k = k_ref[sl, kcols].astype(F32)
            v = v_ref[sl, vcols]
            att = jnp.where(row == col, lax.dot_general(q.astype(BF16), k.astype(BF16), NT_DIMS,
                                                        preferred_element_type=F32), 0.0)
            for lev in range(nlev):
                cpiv = piv_all[lev * chunk:(lev + 1) * chunk, kcols]
                upper = ((trow >> lev) & 1) == 1
                qe = jnp.where(upper, q * jnp.exp(jnp.minimum(bc - cpiv, 0.0)), 0.0)
                ke = jnp.where(upper, 0.0, k * jnp.exp(jnp.minimum(cpiv - bc, 0.0)))
                a_lev = lax.dot_general(qe.astype(BF16), ke.astype(BF16), NT_DIMS,
                                        preferred_element_type=F32)
                att = att + jnp.where((row >> (lev + 1)) == (col >> (lev + 1)), a_lev, 0.0)
            bl = bc[chunk - 1:chunk, :]
            st = st_scr[hh]
            o = jnp.dot(att.astype(BF16), v, preferred_element_type=F32)
            o = o + lax.dot_general((q * jnp.exp(bc)).astype(BF16), st.astype(BF16), NT_DIMS,
                                    preferred_element_type=F32)
            upd = lax.dot_general(v, (k * jnp.exp(bl - bc)).astype(BF16), TN_DIMS,
                                  preferred_element_type=F32)
            st_scr[hh] = st * jnp.exp(bl) + upd
            r = r_ref[sl, vcols].astype(F32)
            o_ref[sl, vcols] = (_rms(o) * gn_ref[...] * _silu(r)).astype(o_ref.dtype)
        return carry

    lax.fori_loop(0, nchunk, body, 0)

    @pl.when(t_id == pl.num_programs(2) - 1)
    def _():
        for hh in range(hb):
            sout_ref[0, hh] = st_scr[hh].T


def _gla(proj, a_lr, w_up_hi, w_up_lo, b_up, gn, s0, mix, *, row0, nb, t_len, heads, dk, dv):
    chunk = min(CHUNK, t_len)
    assert chunk & (chunk - 1) == 0
    tt = _tile(t_len, 8 * chunk, chunk)
    nt = t_len // tt
    rb0 = row0 // tt
    hb = 2 if heads % 2 == 0 else 1
    bk, bv = hb * dk, hb * dv
    nh = heads // hb
    kq, kk, kv, kr = 0, nh, (2 * heads * dk) // bv, (2 * heads * dk) // bv + nh

    def rows(b, h, t):
        return rb0 + b * nt + t

    in_specs = [pl.BlockSpec((tt, bk), lambda b, h, t: (rows(b, h, t), kq + h)),
                pl.BlockSpec((tt, bk), lambda b, h, t: (rows(b, h, t), kk + h)),
                pl.BlockSpec((tt, bv), lambda b, h, t: (rows(b, h, t), kv + h)),
                pl.BlockSpec((tt, bv), lambda b, h, t: (rows(b, h, t), kr + h)),
                pl.BlockSpec((tt, LANES), lambda b, h, t: (rows(b, h, t), 0)),
                pl.BlockSpec((LANES, bk), lambda b, h, t: (0, h)),
                pl.BlockSpec((LANES, bk), lambda b, h, t: (0, h)),
                pl.BlockSpec((1, bk), lambda b, h, t: (0, h)),
                pl.BlockSpec((1, dv), lambda b, h, t: (0, 0)),
                pl.BlockSpec((1, hb, dk, dv), lambda b, h, t: (b, h, 0, 0)),
                pl.BlockSpec(memory_space=pl.ANY)]
    return pl.pallas_call(
        functools.partial(_gla_kernel, chunk=chunk, nchunk=tt // chunk, dk=dk),
        grid=(nb, nh, nt),
        in_specs=in_specs,
        out_specs=[pl.BlockSpec((tt, bv), lambda b, h, t: (rows(b, h, t), h)),
                   pl.BlockSpec((1, hb, dk, dv), lambda b, h, t: (b, h, 0, 0))],
        out_shape=[jax.ShapeDtypeStruct(mix.shape, mix.dtype),
                   jax.ShapeDtypeStruct((nb, heads, dk, dv), F32)],
        scratch_shapes=[pltpu.VMEM((hb, dv, dk), F32)],
        input_output_aliases={10: 0},
        compiler_params=_params("parallel", "parallel", "arbitrary"),
        name="gla",
    )(proj, proj, proj, proj, a_lr, w_up_hi, w_up_lo, b_up, gn, s0, mix)


def _conv_kernel(hc_ref, cg_ref, bg_ref, w_ref, c0_ref, mix_ref, y_ref, st_ref, carry_scr):
    del mix_ref
    t = pl.program_id(2)

    @pl.when(t == 0)
    def _():
        carry_scr[...] = c0_ref[0]

    u = cg_ref[...].astype(F32) * hc_ref[...].astype(F32)
    tt = u.shape[0]
    row = lax.broadcasted_iota(jnp.int32, u.shape, 0)
    prev = carry_scr[...]
    p2 = prev[SUBLANES - 2:SUBLANES - 1, :]
    p1 = prev[SUBLANES - 1:SUBLANES, :]
    u1 = jnp.where(row == 0, p1, pltpu.roll(u, 1, 0))
    u2 = jnp.where(row == 0, p2, jnp.where(row == 1, p1, pltpu.roll(u, 2, 0)))
    w = w_ref[...]
    y = w[0:1, :] * u2 + w[1:2, :] * u1 + w[2:3, :] * u
    y_ref[...] = (bg_ref[...].astype(F32) * y).astype(y_ref.dtype)
    tail = u[tt - SUBLANES:tt, :]
    carry_scr[...] = tail

    @pl.when(t == pl.num_programs(2) - 1)
    def _():
        st_ref[0] = tail


def _conv(proj, conv_w8, c0, mix, *, row0, nb, t_len, col0, width, val_w):
    tt = _tile(t_len, 512)
    tc = _tile(width, 1024, LANES)
    nt = t_len // tt
    rb0 = row0 // tt
    cb0 = col0 // tc
    ncb = width // tc

    def rows(b, t):
        return rb0 + b * nt + t

    return pl.pallas_call(
        _conv_kernel,
        grid=(nb, ncb, nt),
        in_specs=[pl.BlockSpec((tt, tc), lambda b, j, t: (rows(b, t), cb0 + j)),
                  pl.BlockSpec((tt, tc), lambda b, j, t: (rows(b, t), cb0 + ncb + j)),
                  pl.BlockSpec((tt, tc), lambda b, j, t: (rows(b, t), cb0 + 2 * ncb + j)),
                  pl.BlockSpec((SUBLANES, tc), lambda b, j, t: (0, j)),
                  pl.BlockSpec((1, SUBLANES, tc), lambda b, j, t: (b, 0, j)),
                  pl.BlockSpec(memory_space=pl.ANY)],
        out_specs=[pl.BlockSpec((tt, tc), lambda b, j, t: (rows(b, t), val_w // tc + j)),
                   pl.BlockSpec((1, SUBLANES, tc), lambda b, j, t: (b, 0, j))],
        out_shape=[jax.ShapeDtypeStruct(mix.shape, mix.dtype),
                   jax.ShapeDtypeStruct((nb, SUBLANES, width), F32)],
        scratch_shapes=[pltpu.VMEM((SUBLANES, tc), F32)],
        input_output_aliases={5: 0},
        compiler_params=_params("parallel", "parallel", "arbitrary"),
        name="conv",
    )(proj, proj, proj, conv_w8, c0, mix)


def _outproj_kernel(mix_ref, w_ref, x_ref, gate_ref, post_ref, pre2_ref, sc2_ref, sh2_ref,
                    x1_ref, h2_ref, h2p_ref, acc, *, pair):
    kk = pl.program_id(1)

    @pl.when(kk == 0)
    def _():
        acc[...] = jnp.zeros_like(acc)

    acc[...] += jnp.dot(mix_ref[...], w_ref[...], preferred_element_type=F32)

    @pl.when(kk == pl.num_programs(1) - 1)
    def _():
        out = acc[...].reshape(x_ref.shape)
        x1 = x_ref[...] + gate_ref[...] * (_rms(out) * post_ref[...])
        x1_ref[...] = x1
        h2 = _rms(x1) * pre2_ref[...] * (1.0 + sc2_ref[...]) + sh2_ref[...]
        h2 = h2.reshape(acc.shape)
        h2_ref[...] = h2
        h2p_ref[...] = _pack_row(h2, pair)


def _outproj(mix, w_out, x_u, mod_u, post, pre2, pair):
    nu, unit, d = x_u.shape
    kdim = mix.shape[1]
    tu = _tile(nu, 8, 1)
    tm = tu * unit
    tk = _tile(kdim, 1024, LANES)

    def mod(comp):
        return pl.BlockSpec((None, tu, 1, d), lambda i, k: (comp, i, 0, 0))

    vec = pl.BlockSpec((1, d), lambda i, k: (0, 0))
    return pl.pallas_call(
        functools.partial(_outproj_kernel, pair=pair),
        grid=(nu // tu, kdim // tk),
        in_specs=[pl.BlockSpec((tm, tk), lambda i, k: (i, k)),
                  pl.BlockSpec((tk, d), lambda i, k: (k, 0)),
                  pl.BlockSpec((tu, unit, d), lambda i, k: (i, 0, 0)),
                  mod(2), vec, vec, mod(4), mod(3)],
        out_specs=[pl.BlockSpec((tu, unit, d), lambda i, k: (i, 0, 0)),
                   pl.BlockSpec((tm, d), lambda i, k: (i, 0)),
                   pl.BlockSpec((tm, None, d // 2), lambda i, k: (i, 0, 0))],
        out_shape=[jax.ShapeDtypeStruct((nu, unit, d), F32),
                   jax.ShapeDtypeStruct((nu * unit, d), F32),
                   jax.ShapeDtypeStruct((nu * unit, 1, d // 2), U32)],
        scratch_shapes=[pltpu.VMEM((tm, d), F32)],
        compiler_params=_params("parallel", "arbitrary"),
        name="outproj",
    )(mix, w_out, x_u, mod_u, post, pre2, mod_u, mod_u)


def _route_kernel(h_ref, wr_ref, br_ref, e_ref, w_ref, cnt_ref, cnt_scr, *, n_exp):
    i = pl.program_id(0)

    @pl.when(i == 0)
    def _():
        cnt_scr[...] = jnp.zeros_like(cnt_scr)

    tm = h_ref.shape[0]
    logits = jnp.dot(h_ref[...], wr_ref[...], precision=HIGHEST, preferred_element_type=F32)
    scores = jax.nn.sigmoid(logits)
    choice = scores + br_ref[...]
    lane = lax.broadcasted_iota(jnp.int32, (tm, n_exp), 1)
    grp = lane // (n_exp // N_GROUPS)
    neg = -jnp.inf

    def first_argmax(vals):
        m = jnp.max(vals, axis=1, keepdims=True)
        idx = jnp.min(jnp.where(vals == m, lane, n_exp), axis=1, keepdims=True)
        return m, idx

    gscore = []
    for g in range(N_GROUPS):
        vals = jnp.where(grp == g, choice, neg)
        m1, i1 = first_argmax(vals)
        m2 = jnp.max(jnp.where(lane == i1, neg, vals), axis=1, keepdims=True)
        gscore.append(m1 + m2)
    emask = jnp.zeros((tm, n_exp), jnp.bool_)
    for g in range(N_GROUPS):
        beaten = jnp.zeros((tm, 1), jnp.int32)
        for g2 in range(N_GROUPS):
            if g2 == g:
                continue
            wins = (gscore[g2] >= gscore[g]) if g2 < g else (gscore[g2] > gscore[g])
            beaten = beaten + wins.astype(jnp.int32)
        emask = jnp.logical_or(emask, jnp.logical_and(grp == g, beaten < TOPK_GROUPS))

    masked = jnp.where(emask, choice, neg)
    hits = []
    sel = jnp.zeros((tm, n_exp), jnp.bool_)
    for _ in range(TOP_K):
        _, idx = first_argmax(masked)
        hit = lane == idx
        hits.append((idx, hit))
        masked = jnp.where(hit, neg, masked)
        sel = jnp.logical_or(sel, hit)

    wsel = jnp.where(sel, scores, 0.0)
    gate_w = wsel / (jnp.sum(wsel, axis=1, keepdims=True) + 1e-20) * ROUTED_SCALE
    cnt_scr[...] = cnt_scr[...] + jnp.sum(jnp.where(sel, 1.0, 0.0), axis=0, keepdims=True)

    out_lane = lax.broadcasted_iota(jnp.int32, (tm, LANES), 1)
    e_out = jnp.zeros((tm, LANES), jnp.int32)
    w_out = jnp.zeros((tm, LANES), F32)
    for kk, (idx, hit) in enumerate(hits):
        e_out = jnp.where(out_lane == kk, idx, e_out)
        w_out = jnp.where(out_lane == kk, jnp.sum(jnp.where(hit, gate_w, 0.0), axis=1, keepdims=True), w_out)
    e_ref[...] = e_out
    w_ref[...] = w_out
    cnt_ref[...] = cnt_scr[...]


def _route(h2, w_router, b_router):
    n, d = h2.shape
    n_exp = w_router.shape[1]
    tm = _tile(n, 512)
    tok = pl.BlockSpec((tm, LANES), lambda i: (i, 0))
    return pl.pallas_call(
        functools.partial(_route_kernel, n_exp=n_exp),
        grid=(n // tm,),
        in_specs=[pl.BlockSpec((tm, d), lambda i: (i, 0)),
                  pl.BlockSpec((d, n_exp), lambda i: (0, 0)),
                  pl.BlockSpec((1, n_exp), lambda i: (0, 0))],
        out_specs=[tok, tok, pl.BlockSpec((SUBLANES, n_exp), lambda i: (0, 0))],
        out_shape=[jax.ShapeDtypeStruct((n, LANES), jnp.int32),
                   jax.ShapeDtypeStruct((n, LANES), F32),
                   jax.ShapeDtypeStruct((SUBLANES, n_exp), F32)],
        scratch_shapes=[pltpu.VMEM((SUBLANES, n_exp), F32)],
        compiler_params=_params("arbitrary"),
        name="route",
    )(h2, w_router, b_router)


def _experts_kernel(ie_ref, ngr_ref, idx_ref, idxn_ref, h_hbm, wg_ref, wu_ref, wd_ref, y_hbm,
                    raw, xb, a_scr, stage, gsem, ssem, *, f1, f2, n_items, n_tok, pair):
    del ie_ref
    i = pl.program_id(0)
    s = pl.program_id(1)
    ngr = ngr_ref[i]
    tn = wd_ref.shape[1]
    gpg = MOE_GRAN // SUBLANES

    def gather_start(idx, n_gran):
        def body(r8, carry):
            base = r8 * SUBLANES
            for j in range(SUBLANES):
                tok = jnp.minimum(idx[base + j] >> 3, n_tok - 1)
                pltpu.make_async_copy(h_hbm.at[tok], raw.at[r8, pl.ds(j, 1), :], gsem).start()
            return carry
        lax.fori_loop(0, n_gran * gpg, body, 0)

    def scatter_start(n_gran):
        def body(r8, carry):
            base = r8 * SUBLANES
            for j in range(SUBLANES):
                pltpu.make_async_copy(stage.at[r8, pl.ds(j, 1), :], y_hbm.at[idx_ref[base + j]], ssem).start()
            return carry
        lax.fori_loop(0, n_gran * gpg, body, 0)

    def rows_wait(n_gran, src, dst, sem):
        for g in range(MOE_MAX_GRAN):
            @pl.when(g < n_gran)
            def _():
                for _ in range(MOE_GRAN):
                    pltpu.make_async_copy(src, dst, sem).wait()

    raw_row = raw.at[0, pl.ds(0, 1), :]
    stage_row = stage.at[0, pl.ds(0, 1), :]

    @pl.when(jnp.logical_and(i == 0, s == 0))
    def _():
        gather_start(idx_ref, ngr)

    @pl.when(s == 0)
    def _():
        rows_wait(ngr, h_hbm.at[0], raw_row, gsem)
        for g in range(MOE_MAX_GRAN):
            @pl.when(g < ngr)
            def _():
                words = raw[g * gpg:(g + 1) * gpg].reshape(MOE_GRAN, raw.shape[2])
                xb[g * MOE_GRAN:(g + 1) * MOE_GRAN, :] = _unpack_row(words, pair)

        @pl.when(i + 1 < n_items)
        def _():
            gather_start(idxn_ref, ngr_ref[jnp.minimum(i + 1, n_items - 1)])

    for nv in range(1, MOE_MAX_GRAN + 1):
        m = nv * MOE_GRAN

        @pl.when(jnp.logical_and(ngr == nv, s < f1))
        def _():
            x = xb[0:m, :]
            g = jnp.dot(x, wg_ref[...].astype(BF16), preferred_element_type=F32)
            u = jnp.dot(x, wu_ref[...].astype(BF16), preferred_element_type=F32)
            a_scr[s, 0:m, :] = (_silu(g) * u).astype(BF16)

    @pl.when(jnp.logical_and(s == f1, i > 0))
    def _():
        rows_wait(ngr_ref[jnp.maximum(i - 1, 0)], stage_row, y_hbm.at[0], ssem)

    for nv in range(1, MOE_MAX_GRAN + 1):
        m = nv * MOE_GRAN

        @pl.when(jnp.logical_and(ngr == nv, s >= f1))
        def _():
            tf = a_scr.shape[2]
            oc = jnp.dot(a_scr[0, 0:m, :], wd_ref[0:tf, :].astype(BF16), preferred_element_type=F32)
            for f in range(1, f1):
                oc = oc + jnp.dot(a_scr[f, 0:m, :], wd_ref[f * tf:(f + 1) * tf, :].astype(BF16),
                                  preferred_element_type=F32)
            col = pl.multiple_of((s - f1) * (tn // 2), tn // 2)
            stage[0:m // SUBLANES, :, pl.ds(col, tn // 2)] = _pack_row(oc, pair).reshape(
                m // SUBLANES, SUBLANES, tn // 2)

    @pl.when(s == f1 + f2 - 1)
    def _():
        scatter_start(ngr)

    @pl.when(jnp.logical_and(i == n_items - 1, s == f1 + f2 - 1))
    def _():
        rows_wait(ngr, stage_row, y_hbm.at[0], ssem)


def _experts(h2p, item_e, item_ngr, item_idx, w_gate, w_up, w_down, n_out_rows, pair):
    n_tok, _, half = h2p.shape
    _, n_exp, d, de = w_gate.shape
    n_items = item_e.shape[0]
    tf = _tile(de, 256, LANES)
    tn = 2 * pair
    f1, f2 = de // tf, d // tn
    tm = MOE_GRAN * MOE_MAX_GRAN

    def wg_map(i, s, ie, ng):
        return (0, ie[i], 0, jnp.minimum(s, f1 - 1))

    def wd_map(i, s, ie, ng):
        return (0, ie[i], 0, jnp.clip(s - f1, 0, f2 - 1))

    return pl.pallas_call(
        functools.partial(_experts_kernel, f1=f1, f2=f2, n_items=n_items, n_tok=n_tok, pair=pair),
        grid_spec=pltpu.PrefetchScalarGridSpec(
            num_scalar_prefetch=2,
            grid=(n_items, f1 + f2),
            in_specs=[pl.BlockSpec((SMEM_BLOCK,), lambda i, s, ie, ng: (i,), memory_space=pltpu.SMEM),
                      pl.BlockSpec((SMEM_BLOCK,), lambda i, s, ie, ng: (jnp.minimum(i + 1, n_items - 1),),
                                   memory_space=pltpu.SMEM),
                      pl.BlockSpec(memory_space=pl.ANY),
                      pl.BlockSpec((None, None, d, tf), wg_map),
                      pl.BlockSpec((None, None, d, tf), wg_map),
                      pl.BlockSpec((None, None, de, tn), wd_map)],
            out_specs=pl.BlockSpec(memory_space=pl.ANY),
            scratch_shapes=[pltpu.VMEM((tm // SUBLANES, SUBLANES, half), U32),
                            pltpu.VMEM((tm, d), BF16),
                            pltpu.VMEM((f1, tm, tf), BF16),
                            pltpu.VMEM((tm // SUBLANES, SUBLANES, half), U32),
                            pltpu.SemaphoreType.DMA(()),
                            pltpu.SemaphoreType.DMA(())]),
        out_shape=jax.ShapeDtypeStruct((n_out_rows, 1, half), U32),
        compiler_params=_params("arbitrary", "arbitrary"),
        name="experts",
    )(item_e, item_ngr, item_idx, item_idx, h2p, w_gate, w_up, w_down)


def _shared_kernel(h_ref, wg_ref, wu_ref, wd_ref, o_ref, xb, a_scr, *, f1):
    s = pl.program_id(1)

    @pl.when(s == 0)
    def _():
        xb[...] = h_ref[...].astype(BF16)

    @pl.when(s < f1)
    def _():
        x = xb[...]
        g = jnp.dot(x, wg_ref[...], preferred_element_type=F32)
        u = jnp.dot(x, wu_ref[...], preferred_element_type=F32)
        a_scr[s] = (_silu(g) * u).astype(BF16)

    @pl.when(s >= f1)
    def _():
        tf = a_scr.shape[2]
        oc = jnp.dot(a_scr[0], wd_ref[0:tf, :], preferred_element_type=F32)
        for f in range(1, f1):
            oc = oc + jnp.dot(a_scr[f], wd_ref[f * tf:(f + 1) * tf, :], preferred_element_type=F32)
        o_ref[...] = oc


def _shared(h2, wg, wu, wd):
    n, d = h2.shape
    de = wg.shape[1]
    tm = _tile(n, 512)
    tf = _tile(de, 256, LANES)
    tn = _tile(d, 1024, LANES)
    f1, f2 = de // tf, d // tn
    return pl.pallas_call(
        functools.partial(_shared_kernel, f1=f1),
        grid=(n // tm, f1 + f2),
        in_specs=[pl.BlockSpec((tm, d), lambda i, s: (i, 0)),
                  pl.BlockSpec((d, tf), lambda i, s: (0, jnp.minimum(s, f1 - 1))),
                  pl.BlockSpec((d, tf), lambda i, s: (0, jnp.minimum(s, f1 - 1))),
                  pl.BlockSpec((de, tn), lambda i, s: (0, jnp.clip(s - f1, 0, f2 - 1)))],
        out_specs=pl.BlockSpec((tm, tn), lambda i, s: (i, jnp.clip(s - f1, 0, f2 - 1))),
        out_shape=jax.ShapeDtypeStruct((n, d), F32),
        scratch_shapes=[pltpu.VMEM((tm, d), BF16), pltpu.VMEM((f1, tm, tf), BF16)],
        compiler_params=_params("parallel", "arbitrary"),
        name="shared",
    )(h2, wg, wu, wd)


def _combine_kernel(y_hbm, wt_ref, sh_ref, x1_ref, gate_ref, post_ref, o_ref, buf, sem, *, pair, blk0):
    i = pl.program_id(0)
    tb = sh_ref.shape[0]
    na = tb * TOP_K

    def y_copy(step, slot):
        start = pl.multiple_of((blk0 + step) * na, na)
        return pltpu.make_async_copy(y_hbm.at[pl.ds(start, na), 0], buf.at[slot], sem.at[slot])

    @pl.when(i == 0)
    def _():
        y_copy(0, 0).start()

    @pl.when(i + 1 < pl.num_programs(0))
    def _():
        y_copy(i + 1, (i + 1) % 2).start()

    y_copy(i, i % 2).wait()
    r = lax.broadcasted_iota(jnp.int32, (tb, na), 0)
    c = lax.broadcasted_iota(jnp.int32, (tb, na), 1)
    wmat = jnp.where((c >> 3) == r, wt_ref[...], 0.0)
    vals = _unpack_row(buf[i % 2], pair)
    routed = jnp.zeros((tb, vals.shape[1]), F32)
    for piece in _bf16_pieces(wmat, 2):
        routed = routed + jnp.dot(piece.astype(BF16), vals, preferred_element_type=F32)
    f = (sh_ref[...] + routed).reshape(x1_ref.shape)
    o_ref[...] = x1_ref[...] + gate_ref[...] * (_rms(f) * post_ref[...])


def _combine(ycomb, wtile, shared, x1_u, mod_u, post, pair, *, tu, unit0, n_units):
    _, unit, d = x1_u.shape
    tb = tu * unit
    assert TOP_K == SUBLANES and unit0 % tu == 0 and n_units % tu == 0
    blk0 = unit0 // tu
    return pl.pallas_call(
        functools.partial(_combine_kernel, pair=pair, blk0=blk0),
        grid=(n_units // tu,),
        in_specs=[pl.BlockSpec(memory_space=pl.ANY),
                  pl.BlockSpec((tb, tb * TOP_K), lambda i: (blk0 + i, 0)),
                  pl.BlockSpec((tb, d), lambda i: (blk0 + i, 0)),
                  pl.BlockSpec((tu, unit, d), lambda i: (blk0 + i, 0, 0)),
                  pl.BlockSpec((None, tu, 1, d), lambda i: (5, blk0 + i, 0, 0)),
                  pl.BlockSpec((1, d), lambda i: (0, 0))],
        out_specs=pl.BlockSpec((tu, unit, d), lambda i: (i, 0, 0)),
        out_shape=jax.ShapeDtypeStruct((n_units, unit, d), F32),
        scratch_shapes=[pltpu.VMEM((2, tb * TOP_K, d // 2), U32), pltpu.SemaphoreType.DMA((2,))],
        compiler_params=_params("arbitrary"),
        name="combine",
    )(ycomb, wtile, shared, x1_u, mod_u, post)


def _work_items(flat_e, counts, n_assign):
    n_exp = counts.shape[0]
    tm = MOE_GRAN * MOE_MAX_GRAN
    assert n_assign % tm == 0 and tm <= SMEM_BLOCK
    n_items = n_assign // tm + n_exp
    unused = 2 * n_exp + 1
    pad_e = jnp.arange(n_exp, dtype=jnp.int32)[:, None]
    pad_j = jnp.arange(tm, dtype=jnp.int32)[None, :]
    pad_key = jnp.where(pad_j < ((-counts) % tm)[:, None], 2 * pad_e + 1, unused).reshape(n_exp * tm)
    pad_val = jnp.broadcast_to(n_assign + pad_j % MOE_GRAN, (n_exp, tm)).reshape(n_exp * tm)
    keys = jnp.concatenate([2 * flat_e, pad_key])
    vals = jnp.concatenate([jnp.arange(n_assign, dtype=jnp.int32), pad_val])
    keys, vals = lax.sort_key_val(keys, vals)
    keys = keys.reshape(n_items, tm)
    last_e = jnp.max(jnp.where(counts > 0, jnp.arange(n_exp, dtype=jnp.int32), 0))
    item_e = jnp.where(keys[:, 0] < unused, keys[:, 0] >> 1, last_e).astype(jnp.int32)
    cnt = jnp.sum((keys & 1) == 0, axis=1)
    item_ngr = ((cnt + MOE_GRAN - 1) // MOE_GRAN).astype(jnp.int32)
    item_idx = jnp.pad(vals.reshape(n_items, tm), ((0, 0), (0, SMEM_BLOCK - tm)))
    return item_e, item_ngr, item_idx.reshape(n_items * SMEM_BLOCK)


def kernel(x_prompt, x_sample, c_prompt, c_sample, state_gla, state_conv, w_ada, b_ada, norm_mix_pre, norm_mix_post, norm_ffn_pre, norm_ffn_post, w_in, w_decay_up, b_decay_up, gla_norm, conv_w, w_out, w_router, b_router, w_shared_gate, w_shared_up, w_shared_down, w_expert_gate, w_expert_up, w_expert_down):
    bp, tp, d = x_prompt.shape
    bs, ts, _ = x_sample.shape
    depth, _, heads, dk, dv = state_gla.shape
    assert depth == 1 and tp % ts == 0 and ts % SUBLANES == 0 and d % LANES == 0
    unit = ts
    key_w, val_w = heads * dk, heads * dv
    rank = w_decay_up.shape[1]
    cw = conv_w.shape[-1]
    n_p, n_s = bp * tp, bs * ts
    n = n_p + n_s
    nu = n // unit
    pair = min(MOE_PAIR, d // 2)

    x_u = jnp.concatenate([x_prompt.reshape(n_p, d), x_sample.reshape(n_s, d)], axis=0).reshape(nu, unit, d)
    nb = bp + bs
    nb8 = -(-nb // SUBLANES) * SUBLANES
    c_all = jnp.pad(jnp.concatenate([c_prompt, c_sample], axis=0), ((0, nb8 - nb), (0, 0)))
    mod = _ada(c_all, w_ada[0], b_ada)
    unit_batch = jnp.concatenate([jnp.repeat(jnp.arange(bp), tp // unit), bp + jnp.arange(bs)])
    mod_u = mod.reshape(nb8, 6, d)[unit_batch].transpose(1, 0, 2)[:, :, None, :]

    q_end = 2 * key_w + 2 * val_w
    w_in0 = w_in[0]
    w_main = jnp.concatenate([w_in0[:, :q_end], w_in0[:, q_end + rank:]], axis=1).astype(BF16)
    w_a = jnp.pad(w_in0[:, q_end:q_end + rank], ((0, 0), (0, LANES - rank))).astype(BF16)
    proj, a_lr = _inproj(x_u, mod_u, norm_mix_pre, w_main, w_a)
    w_up = jnp.pad(w_decay_up[0], ((0, LANES - rank), (0, 0)))
    w_up_hi = w_up.astype(BF16)
    w_up_lo = (w_up - w_up_hi.astype(F32)).astype(BF16)
    gla_kw = dict(heads=heads, dk=dk, dv=dv)
    mix = jnp.zeros((n, val_w + cw), BF16)
    mix, gla_p = _gla(proj, a_lr, w_up_hi, w_up_lo, b_decay_up, gla_norm, jnp.zeros((bp, heads, dk, dv), F32),
                      mix, row0=0, nb=bp, t_len=tp, **gla_kw)
    mix, gla_s = _gla(proj, a_lr, w_up_hi, w_up_lo, b_decay_up, gla_norm, state_gla[0], mix,
                      row0=n_p, nb=bs, t_len=ts, **gla_kw)
    conv_w8 = jnp.pad(conv_w[0], ((0, SUBLANES - conv_w.shape[1]), (0, 0)))
    tail = SUBLANES - state_conv.shape[2]
    conv_kw = dict(col0=q_end, width=cw, val_w=val_w)
    mix, conv_p = _conv(proj, conv_w8, jnp.zeros((bp, SUBLANES, cw), F32), mix,
                        row0=0, nb=bp, t_len=tp, **conv_kw)
    mix, conv_s = _conv(proj, conv_w8, jnp.pad(state_conv[0], ((0, 0), (tail, 0), (0, 0))), mix,
                        row0=n_p, nb=bs, t_len=ts, **conv_kw)
    x1_u, h2, h2p = _outproj(mix, w_out[0].astype(BF16), x_u, mod_u, norm_mix_post, norm_ffn_pre, pair)

    e_sel, w_sel, counts8 = _route(h2, w_router[0], b_router)
    n_assign = n * TOP_K
    flat_e = e_sel[:, :TOP_K].reshape(n_assign)
    item_e, item_ngr, item_idx = _work_items(flat_e, counts8[0].astype(jnp.int32), n_assign)
    ycomb = _experts(h2p, item_e, item_ngr, item_idx, w_expert_gate, w_expert_up, w_expert_down,
                     n_assign + MOE_GRAN, pair)
    shared = _shared(h2, w_shared_gate[0].astype(BF16), w_shared_up[0].astype(BF16),
                     w_shared_down[0].astype(BF16))
    tu = _tile(n_s // unit, 4, 1)
    wtile = jnp.tile(w_sel[:, :TOP_K], (1, tu * unit))
    comb = (ycomb, wtile, shared, x1_u, mod_u, norm_ffn_post, pair)
    y_p = _combine(*comb, tu=tu, unit0=0, n_units=n_p // unit)
    y_s = _combine(*comb, tu=tu, unit0=n_p // unit, n_units=n_s // unit)
    return (y_p.reshape(bp, tp, d), y_s.reshape(bs, ts, d),
            gla_p[None], conv_p[None, :, tail:, :], gla_s[None], conv_s[None, :, tail:, :])
```
